```python
import math
import jax, jax.numpy as jnp
from jax import lax
import numpy as np

D_MODEL = 2048
BATCH = 2
SEQ = 4096
DEPTH = 2

CHUNK = 64
D_MIX = D_MODEL
D_POOL = D_MIX // 2
POOL_WINDOWS = (2, 4, 8, 16)
N_POOL_GROUPS = len(POOL_WINDOWS)
POOL_GROUP = D_POOL // N_POOL_GROUPS
D_ATTN = D_MIX - D_POOL
HEAD_DIM = 64
N_Q_HEADS = D_ATTN // HEAD_DIM
N_KV_HEADS = 2
Q_PER_KV = N_Q_HEADS // N_KV_HEADS
WINDOW = 128
WINDOW_CHUNKS = WINDOW // CHUNK
BAND_CHUNKS = WINDOW_CHUNKS + 1
BAND = BAND_CHUNKS * CHUNK
N_BUCKETS = 32
MAX_DISTANCE = 128
D_IN = D_POOL + (N_Q_HEADS + 2 * N_KV_HEADS) * HEAD_DIM
PEER_HEADS = 8
N_KEYS = 128
N_EXPERTS = N_KEYS * N_KEYS
PEER_TOPK = 16
D_QUERY = 256
D_HALF = D_QUERY // 2
PEER_TOKEN_BLOCK = 128
EPS = 1e-6

kernel_name = 'hymba_pool_swa_peer_block'


def rms_norm(x, gain):
    xf = x.astype(jnp.float32)
    y = xf * lax.rsqrt(jnp.mean(xf * xf, axis=-1, keepdims=True) + EPS)
    return (y * gain.astype(jnp.float32)).astype(x.dtype)


def t5_bucket(rel):
    nb = N_BUCKETS // 2
    max_exact = nb // 2
    ret = jnp.where(rel > 0, nb, 0)
    n = jnp.abs(rel)
    nf = jnp.maximum(n, 1).astype(jnp.float32)
    large = max_exact + (jnp.log(nf / max_exact) / math.log(MAX_DISTANCE / max_exact) * (nb - max_exact)).astype(jnp.int32)
    large = jnp.minimum(large, nb - 1)
    return ret + jnp.where(n < max_exact, n, large)


def pool_mixer(p, w_pool, pool_scale):
    B, S, _ = p.shape
    pf = p.astype(jnp.float32).reshape(B, S, N_POOL_GROUPS, POOL_GROUP)
    cs = jnp.cumsum(pf, axis=1)
    t = jnp.arange(S)
    means = []
    for g, w in enumerate(POOL_WINDOWS):
        c_g = cs[:, :, g]
        prev = jnp.pad(c_g, ((0, 0), (w, 0), (0, 0)))[:, :S]
        cnt = jnp.minimum(t + 1, w).astype(jnp.float32)[None, :, None]
        means.append((c_g - prev) / cnt)
    pooled = (jnp.stack(means, axis=2) - pf).astype(p.dtype)
    y = jnp.einsum('bsgc,gcd->bsgd', pooled, w_pool)
    return y.reshape(B, S, D_POOL) * pool_scale


def swa_sink_attention(q, k, v, q_gain, k_gain, sinks, rel_bias):
    B, S = q.shape[:2]
    NC = S // CHUNK
    q = rms_norm(q, q_gain)
    k = rms_norm(k, k_gain)
    qc = q.reshape(B, NC, CHUNK, N_KV_HEADS, Q_PER_KV, HEAD_DIM)

    def band(t):
        tc = t.reshape(B, NC, CHUNK, N_KV_HEADS, HEAD_DIM)
        tp = jnp.pad(tc, ((0, 0), (WINDOW_CHUNKS, 0), (0, 0), (0, 0), (0, 0)))
        return jnp.concatenate([tp[:, j:j + NC] for j in range(BAND_CHUNKS)], axis=2)

    kb = band(k)
    vb = band(v)
    scores = jnp.einsum('bnqhgd,bnkhd->bnhgqk', qc, kb, preferred_element_type=jnp.float32) * (1.0 / math.sqrt(HEAD_DIM))
    a = jnp.arange(CHUNK)[:, None]
    bk = jnp.arange(BAND)[None, :]
    rel = bk - WINDOW_CHUNKS * CHUNK - a
    bias = rel_bias.astype(jnp.float32)[t5_bucket(rel)]
    bias = bias.transpose(2, 0, 1).reshape(N_KV_HEADS, Q_PER_KV, CHUNK, BAND)
    key_chunk = jnp.arange(NC)[:, None] - WINDOW_CHUNKS + bk // CHUNK
    valid = key_chunk >= 0
    scores = jnp.where(valid[None, :, None, None, None, :], scores + bias, -jnp.inf)
    sink = jnp.broadcast_to(sinks.astype(jnp.float32).reshape(N_KV_HEADS, Q_PER_KV, 1, 1), scores.shape[:-1] + (1,))
    probs = jax.nn.softmax(jnp.concatenate([scores, sink], axis=-1), axis=-1)[..., :-1]
    out = jnp.einsum('bnhgqk,bnkhd->bnqhgd', probs.astype(v.dtype), vb)
    return out.reshape(B, S, D_ATTN)


def hybrid_mixer(h, w_in, q_gain, k_gain, sinks, rel_bias, w_pool, pool_scale, w_out):
    B, S, _ = h.shape
    z = h @ w_in
    p = z[..., :D_POOL]
    o = D_POOL
    q = z[..., o:o + D_ATTN].reshape(B, S, N_Q_HEADS, HEAD_DIM)
    o += D_ATTN
    k = z[..., o:o + N_KV_HEADS * HEAD_DIM].reshape(B, S, N_KV_HEADS, HEAD_DIM)
    o += N_KV_HEADS * HEAD_DIM
    v = z[..., o:o + N_KV_HEADS * HEAD_DIM].reshape(B, S, N_KV_HEADS, HEAD_DIM)
    y_pool = pool_mixer(p, w_pool, pool_scale)
    y_attn = swa_sink_attention(q, k, v, q_gain, k_gain, sinks, rel_bias)
    return jnp.concatenate([y_pool, y_attn], axis=-1) @ w_out


def peer_ffn(h, w_query, sub_keys, expert_u, expert_v):
    B, S, D = h.shape
    q = (h @ w_query).reshape(B, S, PEER_HEADS, 2, D_HALF)
    s = jnp.einsum('bshpd,pnd->bshpn', q, sub_keys).astype(jnp.float32)
    s1, i1 = lax.top_k(s[..., 0, :], PEER_TOPK)
    s2, i2 = lax.top_k(s[..., 1, :], PEER_TOPK)
    cand = (s1[..., :, None] + s2[..., None, :]).reshape(B, S, PEER_HEADS, PEER_TOPK * PEER_TOPK)
    cand_idx = (i1[..., :, None] * N_KEYS + i2[..., None, :]).reshape(B, S, PEER_HEADS, PEER_TOPK * PEER_TOPK)
    top_s, pos = lax.top_k(cand, PEER_TOPK)
    idx = jnp.take_along_axis(cand_idx, pos, axis=-1)
    gate = jax.nn.softmax(top_s, axis=-1)
    n_blk = (B * S) // PEER_TOKEN_BLOCK
    hb = h.reshape(n_blk, PEER_TOKEN_BLOCK, D)
    ib = idx.reshape(n_blk, PEER_TOKEN_BLOCK, PEER_HEADS, PEER_TOPK)
    gb = gate.reshape(n_blk, PEER_TOKEN_BLOCK, PEER_HEADS, PEER_TOPK)

    def block(args):
        hx, ix, gx = args
        u = expert_u[ix]
        act = jnp.einsum('thkd,td->thk', u, hx, preferred_element_type=jnp.float32)
        act = (jax.nn.gelu(act, approximate=False) * gx).astype(hx.dtype)
        vv = expert_v[ix]
        return jnp.einsum('thk,thkd->td', act, vv)

    out = lax.map(block, (hb, ib, gb))
    return out.reshape(B, S, D)


def setup_inputs(seed: int = 0) -> dict:
    key = jax.random.key(seed)
    ks = jax.random.split(key, 18)
    f32 = jnp.float32

    def nrm(k, shape, s):
        return jax.random.normal(k, shape, f32) * s

    return {
        'x': nrm(ks[0], (BATCH, SEQ, D_MODEL), 1.0),
        'c': nrm(ks[1], (BATCH, D_MODEL), 1.0),
        'rel_bias': nrm(ks[2], (N_BUCKETS, N_Q_HEADS), 0.5),
        'ada_w': nrm(ks[3], (DEPTH, D_MODEL, 6 * D_MODEL), 0.5 * D_MODEL ** -0.5),
        'ada_b': nrm(ks[4], (DEPTH, 6 * D_MODEL), 0.02),
        'norm1_gain': 1.0 + nrm(ks[5], (DEPTH, D_MODEL), 0.1),
        'w_in': nrm(ks[6], (DEPTH, D_MODEL, D_IN), D_MODEL ** -0.5),
        'q_gain': 1.0 + nrm(ks[7], (DEPTH, HEAD_DIM), 0.1),
        'k_gain': 1.0 + nrm(ks[8], (DEPTH, HEAD_DIM), 0.1),
        'sinks': nrm(ks[9], (DEPTH, N_Q_HEADS), 0.5),
        'w_pool': nrm(ks[10], (DEPTH, N_POOL_GROUPS, POOL_GROUP, POOL_GROUP), POOL_GROUP ** -0.5),
        'pool_scale': 1.0 + nrm(ks[11], (DEPTH, D_POOL), 0.1),
        'w_out': nrm(ks[12], (DEPTH, D_MIX, D_MODEL), D_MIX ** -0.5),
        'norm2_gain': 1.0 + nrm(ks[13], (DEPTH, D_MODEL), 0.1),
        'peer_w_query': nrm(ks[14], (DEPTH, D_MODEL, PEER_HEADS * D_QUERY), D_MODEL ** -0.5),
        'peer_sub_keys': nrm(ks[15], (DEPTH, 2, N_KEYS, D_HALF), D_HALF ** -0.5),
        'peer_u': nrm(ks[16], (DEPTH, N_EXPERTS, D_MODEL), D_MODEL ** -0.5),
        'peer_v': nrm(ks[17], (DEPTH, N_EXPERTS, D_MODEL), 1.0),
    }


def reference(x, c, rel_bias, ada_w, ada_b, norm1_gain, w_in, q_gain, k_gain, sinks, w_pool, pool_scale,
              w_out, norm2_gain, peer_w_query, peer_sub_keys, peer_u, peer_v):
    cond = jax.nn.silu(c)
    for l in range(DEPTH):
        mod = cond @ ada_w[l] + ada_b[l]
        sh1, sc1, g1, sh2, sc2, g2 = [m[:, None, :] for m in jnp.split(mod, 6, axis=-1)]
        h = rms_norm(x, norm1_gain[l]) * (1 + sc1) + sh1
        x = x + g1 * hybrid_mixer(h, w_in[l], q_gain[l], k_gain[l], sinks[l], rel_bias,
                                  w_pool[l], pool_scale[l], w_out[l])
        h = rms_norm(x, norm2_gain[l]) * (1 + sc2) + sh2
        x = x + g2 * peer_ffn(h, peer_w_query[l], peer_sub_keys[l], peer_u[l], peer_v[l])
    return x
```

```python
import functools
import math

import jax
import jax.numpy as jnp
import numpy as np
from jax import lax
from jax.experimental import pallas as pl
from jax.experimental.pallas import tpu as pltpu

F32 = jnp.float32
BF16 = jnp.bfloat16

D_MODEL = 2048
CHUNK = 64
D_POOL = 1024
POOL_WINDOWS = (2, 4, 8, 16)
POOL_GROUP = D_POOL // len(POOL_WINDOWS)
D_ATTN = 1024
HEAD_DIM = 64
N_Q_HEADS = 16
N_KV_HEADS = 2
Q_PER_KV = N_Q_HEADS // N_KV_HEADS
WINDOW = 128
N_BUCKETS = 32
MAX_DISTANCE = 128
D_KV = N_KV_HEADS * HEAD_DIM
D_IN = D_POOL + D_ATTN + 2 * D_KV
PEER_HEADS = 8
N_KEYS = 128
N_EXPERTS = N_KEYS * N_KEYS
PEER_TOPK = 16
D_QUERY = 256
D_HALF = D_QUERY // 2
EPS = 1e-6

LANES = 128
SUBLANES = 8

ADA_TN = 1024
Q_BLOCK = 256
K_BLOCK = Q_BLOCK + WINDOW
POOL_HALO = 16
ROUTE_BLOCK = 256
EXP_TM = 1024
EXP_TE = 512
FINAL_BLOCK = 512

NEG_INF = float("-inf")


def _rms(xf, gain):
    return xf * lax.rsqrt(jnp.mean(xf * xf, axis=-1, keepdims=True) + EPS) * gain


def _ada_kernel(ct_ref, w_ref, b_ref, o_ref):
    w = w_ref[0]
    n_batch = ct_ref.shape[1]
    for b in range(n_batch):
        cb = ct_ref[:, b:b + 1]
        cb = cb * jax.nn.sigmoid(cb)
        o_ref[0, b:b + 1, :] = jnp.sum(w * cb, axis=0, keepdims=True) + b_ref[0]


def _ada_call(c, ada_w, ada_b):
    depth, d, n = ada_w.shape
    batch = c.shape[0]
    return pl.pallas_call(
        _ada_kernel,
        grid=(depth, n // ADA_TN),
        in_specs=[
            pl.BlockSpec((d, batch), lambda l, j: (0, 0)),
            pl.BlockSpec((1, d, ADA_TN), lambda l, j: (l, 0, j)),
            pl.BlockSpec((1, 1, ADA_TN), lambda l, j: (l, 0, j)),
        ],
        out_specs=pl.BlockSpec((1, batch, ADA_TN), lambda l, j: (l, 0, j)),
        out_shape=jax.ShapeDtypeStruct((depth, batch, n), F32),
        compiler_params=pltpu.CompilerParams(
            dimension_semantics=("arbitrary", "arbitrary"), vmem_limit_bytes=40 * 2**20),
        name="ada_mod",
    )(c.T, ada_w, ada_b.reshape(depth, 1, n))


def _t5_bucket(rel):
    nb = N_BUCKETS // 2
    max_exact = nb // 2
    ret = jnp.where(rel > 0, nb, 0)
    n = jnp.abs(rel)
    nf = jnp.maximum(n, 1).astype(jnp.float32)
    large = max_exact + (jnp.log(nf / max_exact) / math.log(MAX_DISTANCE / max_exact) * (nb - max_exact)).astype(jnp.int32)
    large = jnp.minimum(large, nb - 1)
    return ret + jnp.where(n < max_exact, n, large)


def _bias_kernel(rb_ref, bucket_ref, valid_ref, o_ref):
    h = pl.program_id(0)
    bucket = bucket_ref[...]
    acc = jnp.zeros(bucket.shape, F32)
    for b in range(N_BUCKETS):
        acc = jnp.where(bucket == b, rb_ref[b, h], acc)
    o_ref[0] = jnp.where(valid_ref[...] > 0, acc, NEG_INF)


def _bias_call(rel_bias):
    a = jnp.arange(Q_BLOCK)[:, None]
    bk = jnp.arange(K_BLOCK)[None, :]
    rel = bk - WINDOW - a
    bucket = _t5_bucket(rel).astype(jnp.int32)
    dchunk = (bk - WINDOW) // CHUNK - a // CHUNK
    valid = ((dchunk >= -(WINDOW // CHUNK)) & (dchunk <= 0)).astype(jnp.int32)
    return pl.pallas_call(
        _bias_kernel,
        grid=(N_Q_HEADS,),
        in_specs=[
            pl.BlockSpec(memory_space=pltpu.SMEM),
            pl.BlockSpec((Q_BLOCK, K_BLOCK), lambda h: (0, 0)),
            pl.BlockSpec((Q_BLOCK, K_BLOCK), lambda h: (0, 0)),
        ],
        out_specs=pl.BlockSpec((1, Q_BLOCK, K_BLOCK), lambda h: (h, 0, 0)),
        out_shape=jax.ShapeDtypeStruct((N_Q_HEADS, Q_BLOCK, K_BLOCK), F32),
        compiler_params=pltpu.CompilerParams(dimension_semantics=("arbitrary",)),
        name="rel_bias_tile",
    )(rel_bias.astype(F32), bucket, valid)


def _mix_kernel(has_acc, *refs):
    if has_acc:
        (x_ref, acc_ref, modp_ref, mod_ref, n1g_ref, w_in_ref, qg_ref, kg_ref, sinks_ref, bias_ref,
         w_pool_ref, ps_ref, w_out_ref, o_ref, p_ext, k_ext, v_ext, y_scr) = refs
    else:
        (x_ref, mod_ref, n1g_ref, w_in_ref, qg_ref, kg_ref, sinks_ref, bias_ref,
         w_pool_ref, ps_ref, w_out_ref, o_ref, p_ext, k_ext, v_ext, y_scr) = refs
    i = pl.program_id(1)
    x = x_ref[0]
    if has_acc:
        x = x + modp_ref[0, 5:6, :] * acc_ref[0]
    sh1 = mod_ref[0, 0:1, :]
    sc1 = mod_ref[0, 1:2, :]
    g1 = mod_ref[0, 2:3, :]
    h = _rms(x, n1g_ref[...]) * (1.0 + sc1) + sh1
    z = jnp.dot(h.astype(BF16), w_in_ref[...], preferred_element_type=F32)

    @pl.when(i == 0)
    def _():
        p_ext[0:POOL_HALO, :] = jnp.zeros((POOL_HALO, D_POOL), F32)
        k_ext[0:WINDOW, :] = jnp.zeros((WINDOW, D_KV), F32)
        v_ext[0:WINDOW, :] = jnp.zeros((WINDOW, D_KV), F32)

    @pl.when(i > 0)
    def _():
        p_ext[0:POOL_HALO, :] = p_ext[Q_BLOCK:Q_BLOCK + POOL_HALO, :]
        k_ext[0:WINDOW, :] = k_ext[Q_BLOCK:Q_BLOCK + WINDOW, :]
        v_ext[0:WINDOW, :] = v_ext[Q_BLOCK:Q_BLOCK + WINDOW, :]

    p_ext[POOL_HALO:POOL_HALO + Q_BLOCK, :] = z[:, 0:D_POOL]
    kg = kg_ref[...]
    for g in range(N_KV_HEADS):
        lo = D_POOL + D_ATTN + g * HEAD_DIM
        k_ext[WINDOW:WINDOW + Q_BLOCK, g * HEAD_DIM:(g + 1) * HEAD_DIM] = _rms(z[:, lo:lo + HEAD_DIM], kg)
    v_ext[WINDOW:WINDOW + Q_BLOCK, :] = z[:, D_POOL + D_ATTN + D_KV:D_IN]

    tpos = i * Q_BLOCK + lax.broadcasted_iota(jnp.int32, (Q_BLOCK, 1), 0)
    for g, w in enumerate(POOL_WINDOWS):
        c0 = g * POOL_GROUP
        cur = p_ext[POOL_HALO:POOL_HALO + Q_BLOCK, c0:c0 + POOL_GROUP]
        acc = cur
        for d in range(1, w):
            acc = acc + p_ext[POOL_HALO - d:POOL_HALO - d + Q_BLOCK, c0:c0 + POOL_GROUP]
        cnt = jnp.minimum(tpos + 1, w).astype(F32)
        pooled = acc / cnt - cur
        y = jnp.dot(pooled.astype(BF16), w_pool_ref[g], preferred_element_type=F32)
        y_scr[:, c0:c0 + POOL_GROUP] = y * ps_ref[:, c0:c0 + POOL_GROUP]

    qg = qg_ref[...]
    first_cols = lax.broadcasted_iota(jnp.int32, (Q_BLOCK, K_BLOCK), 1) < WINDOW
    hide = jnp.logical_and(first_cols, i == 0)
    for g in range(N_KV_HEADS):
        kn = k_ext[:, g * HEAD_DIM:(g + 1) * HEAD_DIM].astype(BF16)
        vv = v_ext[:, g * HEAD_DIM:(g + 1) * HEAD_DIM].astype(BF16)
        for hq in range(g * Q_PER_KV, (g + 1) * Q_PER_KV):
            lo = D_POOL + hq * HEAD_DIM
            qh = _rms(z[:, lo:lo + HEAD_DIM], qg).astype(BF16)
            s = lax.dot_general(qh, kn, (((1,), (1,)), ((), ())), preferred_element_type=F32)
            s = s * (1.0 / math.sqrt(HEAD_DIM)) + bias_ref[hq]
            s = jnp.where(hide, NEG_INF, s)
            sink = sinks_ref[hq]
            m = jnp.maximum(jnp.max(s, axis=-1, keepdims=True), sink)
            e = jnp.exp(s - m)
            den = jnp.sum(e, axis=-1, keepdims=True) + jnp.exp(sink - m)
            o = jnp.dot(e.astype(BF16), vv, preferred_element_type=F32) / den
            y_scr[:, D_POOL + hq * HEAD_DIM:D_POOL + (hq + 1) * HEAD_DIM] = o

    out = jnp.dot(y_scr[...].astype(BF16), w_out_ref[...], preferred_element_type=F32)
    o_ref[0] = x + g1 * out


def _const_spec(shape):
    nd = len(shape)
    return pl.BlockSpec(shape, lambda *_: (0,) * nd, pipeline_mode=pl.Buffered(1))


def _mix_call(x, acc, mod_prev, mod, n1g, w_in, qg, kg, sinks, bias, w_pool, ps, w_out):
    batch, seq, d = x.shape
    has_acc = acc is not None
    tok_spec = pl.BlockSpec((1, Q_BLOCK, d), lambda b, i: (b, i, 0))
    mod_spec = pl.BlockSpec((1, 6, d), lambda b, i: (b, 0, 0))
    in_specs = [tok_spec]
    args = [x]
    if has_acc:
        in_specs += [tok_spec, mod_spec]
        args += [acc, mod_prev]
    in_specs += [
        mod_spec,
        _const_spec((1, d)),
        _const_spec((d, D_IN)),
        _const_spec((1, HEAD_DIM)),
        _const_spec((1, HEAD_DIM)),
        pl.BlockSpec(memory_space=pltpu.SMEM),
        _const_spec((N_Q_HEADS, Q_BLOCK, K_BLOCK)),
        _const_spec((len(POOL_WINDOWS), POOL_GROUP, POOL_GROUP)),
        _const_spec((1, D_POOL)),
        _const_spec((d, d)),
    ]
    args += [mod, n1g, w_in, qg, kg, sinks, bias, w_pool, ps, w_out]
    return pl.pallas_call(
        functools.partial(_mix_kernel, has_acc),
        grid=(batch, seq // Q_BLOCK),
        in_specs=in_specs,
        out_specs=tok_spec,
        out_shape=jax.ShapeDtypeStruct((batch, seq, d), F32),
        scratch_shapes=[
            pltpu.VMEM((POOL_HALO + Q_BLOCK, D_POOL), F32),
            pltpu.VMEM((K_BLOCK, D_KV), F32),
            pltpu.VMEM((K_BLOCK, D_KV), F32),
            pltpu.VMEM((Q_BLOCK, d), F32),
        ],
        compiler_params=pltpu.CompilerParams(
            dimension_semantics=("arbitrary", "arbitrary"), vmem_limit_bytes=56 * 2**20),
        name="mixer",
    )(*args)


def _top16_rows(s):
    rows = lax.broadcasted_iota(jnp.int32, s.shape, 0).astype(F32)
    rank = jnp.full(s.shape, float(PEER_TOPK), F32)
    cur = s
    vals = []
    for r in range(PEER_TOPK):
        m = jnp.max(cur, axis=0, keepdims=True)
        first = jnp.min(jnp.where(cur == m, rows, float(N_KEYS)), axis=0, keepdims=True)
        sel = rows == first
        rank = jnp.where(sel, float(r), rank)
        cur = jnp.where(sel, NEG_INF, cur)
        vals.append(m)
    return rank, vals


def _stack16(vals):
    t = vals[0].shape[1]
    rows = lax.broadcasted_iota(jnp.int32, (PEER_TOPK, t), 0)
    out = jnp.zeros((PEER_TOPK, t), F32)
    for j, v in enumerate(vals):
        out = jnp.where(rows == j, v, out)
    return out


def _select_counts(xs, ys):
    t = xs[0].shape[1]
    rows = lax.broadcasted_iota(jnp.int32, (PEER_TOPK, t), 0)
    rows_f = rows.astype(F32)
    x16 = _stack16(xs)
    y16 = _stack16(ys)
    cur, flat = [], []
    n_a = 4
    big = float(PEER_TOPK * PEER_TOPK)
    for k in range(n_a):
        ok = rows < PEER_TOPK // (k + 1)
        cur.append(jnp.where(ok, x16 + ys[k], NEG_INF))
        flat.append(jnp.where(ok, rows_f * float(PEER_TOPK) + float(k), big + 1.0))
    for j in range(3):
        ok = jnp.logical_and(rows >= n_a, rows < PEER_TOPK // (j + 1))
        cur.append(jnp.where(ok, y16 + xs[j], NEG_INF))
        flat.append(jnp.where(ok, rows_f + float(PEER_TOPK * j), big + 1.0))
    taken = [jnp.zeros((PEER_TOPK, t), F32) for _ in cur]
    top = xs[0] + ys[0]
    z = jnp.zeros((1, t), F32)
    for _ in range(PEER_TOPK):
        m = cur[0]
        for c in cur[1:]:
            m = jnp.maximum(m, c)
        m = jnp.max(m, axis=0, keepdims=True)
        f = jnp.where(cur[0] == m, flat[0], big)
        for c, fl in zip(cur[1:], flat[1:]):
            f = jnp.minimum(f, jnp.where(c == m, fl, big))
        f = jnp.min(f, axis=0, keepdims=True)
        for n in range(len(cur)):
            sel = flat[n] == f
            cur[n] = jnp.where(sel, NEG_INF, cur[n])
            taken[n] = jnp.where(sel, 1.0, taken[n])
        z = z + jnp.exp(m - top)
    cnt = taken[0]
    for k in range(1, n_a):
        cnt = cnt + taken[k]
    for j in range(3):
        cnt = cnt + jnp.where(rows == j, jnp.sum(taken[n_a + j], axis=0, keepdims=True), 0.0)
    return cnt, z


def _route_kernel(x_ref, mod_ref, n2g_ref, wq_ref, keys_ref, h2_ref, r2_ref, e2_ref, c_ref, w1_ref, q_scr):
    x = x_ref[0]
    sh2 = mod_ref[0, 3:4, :]
    sc2 = mod_ref[0, 4:5, :]
    h2 = (_rms(x, n2g_ref[...]) * (1.0 + sc2) + sh2).astype(BF16)
    h2_ref[0] = h2
    q_scr[...] = jnp.dot(h2, wq_ref[...], preferred_element_type=F32)
    k1 = keys_ref[0]
    k2 = keys_ref[1]
    t = q_scr.shape[0]

    def head(hd, carry):
        off = pl.multiple_of(hd * D_QUERY, D_QUERY)
        q1 = q_scr[:, pl.ds(off, D_HALF)].astype(BF16)
        q2 = q_scr[:, pl.ds(off + D_HALF, D_HALF)].astype(BF16)
        dn = (((1,), (1,)), ((), ()))
        s1 = lax.dot_general(k1, q1, dn, preferred_element_type=F32)
        s2 = lax.dot_general(k2, q2, dn, preferred_element_type=F32)
        r1, xs = _top16_rows(s1)
        r2, ys = _top16_rows(s2)
        cnt, z = _select_counts(xs, ys)
        c_full = jnp.zeros(s1.shape, F32)
        for j in range(PEER_TOPK):
            c_full = jnp.where(r1 == float(j), cnt[j:j + 1, :], c_full)
        w1 = jnp.exp(s1 - xs[0]) / z
        r2_ref[hd] = r2
        e2_ref[hd] = jnp.exp(s2 - ys[0])
        c_ref[:, hd, :, :] = c_full.reshape(N_KEYS // SUBLANES, SUBLANES, t)
        w1_ref[:, hd, :, :] = w1.reshape(N_KEYS // SUBLANES, SUBLANES, t)
        return carry

    lax.fori_loop(0, PEER_HEADS, head, 0)


def _route_call(x1, mod, n2g, wq, keys):
    batch, seq, d = x1.shape
    n_tok = batch * seq
    nblk = seq // ROUTE_BLOCK
    tok_spec = pl.BlockSpec((1, ROUTE_BLOCK, d), lambda b, i: (b, i, 0))
    lane_spec = pl.BlockSpec((PEER_HEADS, N_KEYS, ROUTE_BLOCK), lambda b, i: (0, 0, b * nblk + i))
    row_spec = pl.BlockSpec((N_KEYS // SUBLANES, PEER_HEADS, SUBLANES, ROUTE_BLOCK),
                            lambda b, i: (0, 0, 0, b * nblk + i))
    lane_shape = jax.ShapeDtypeStruct((PEER_HEADS, N_KEYS, n_tok), F32)
    row_shape = jax.ShapeDtypeStruct((N_KEYS // SUBLANES, PEER_HEADS, SUBLANES, n_tok), F32)
    return pl.pallas_call(
        _route_kernel,
        grid=(batch, nblk),
        in_specs=[
            tok_spec,
            pl.BlockSpec((1, 6, d), lambda b, i: (b, 0, 0)),
            _const_spec((1, d)),
            _const_spec((d, PEER_HEADS * D_QUERY)),
            _const_spec((2, N_KEYS, D_HALF)),
        ],
        out_specs=[tok_spec, lane_spec, lane_spec, row_spec, row_spec],
        out_shape=[jax.ShapeDtypeStruct((batch, seq, d), BF16), lane_shape, lane_shape, row_shape, row_shape],
        scratch_shapes=[pltpu.VMEM((ROUTE_BLOCK, PEER_HEADS * D_QUERY), F32)],
        compiler_params=pltpu.CompilerParams(
            dimension_semantics=("arbitrary", "arbitrary"), vmem_limit_bytes=48 * 2**20),
        name="peer_route",
    )(x1, mod, n2g, wq, keys)


def _expert_kernel(h2_ref, u_ref, v_ref, r2_ref, e2_ref, c_ref, w1_ref, o_ref, act_scr, g_scr):
    e = pl.program_id(1)
    n_i1 = EXP_TE // N_KEYS
    u = u_ref[...].astype(BF16)
    act_scr[...] = lax.dot_general(u, h2_ref[...], (((1,), (1,)), ((), ())), preferred_element_type=F32)
    row0 = (e % (SUBLANES // n_i1)) * n_i1
    for a in range(n_i1):
        gate = jnp.zeros((N_KEYS, EXP_TM), F32)
        for hd in range(PEER_HEADS):
            cnt = c_ref[0, hd, pl.ds(row0 + a, 1), :]
            w1 = w1_ref[0, hd, pl.ds(row0 + a, 1), :]
            gate = gate + jnp.where(r2_ref[hd] < cnt, e2_ref[hd] * w1, 0.0)
        act = act_scr[a * N_KEYS:(a + 1) * N_KEYS, :]
        gl = 0.5 * act * (1.0 + lax.erf(act * (1.0 / math.sqrt(2.0))))
        g_scr[a * N_KEYS:(a + 1) * N_KEYS, :] = (gl * gate).astype(BF16)
    contrib = lax.dot_general(g_scr[...], v_ref[...].astype(BF16), (((0,), (0,)), ((), ())),
                              preferred_element_type=F32)

    @pl.when(e == 0)
    def _():
        o_ref[...] = contrib

    @pl.when(e > 0)
    def _():
        o_ref[...] += contrib


def _expert_call(h2, u, v, r2, e2, c, w1):
    n_tok, d = h2.shape
    n_i1 = EXP_TE // N_KEYS
    per_blk = SUBLANES // n_i1
    lane_spec = pl.BlockSpec((PEER_HEADS, N_KEYS, EXP_TM), lambda t, e: (0, 0, t), pipeline_mode=pl.Buffered(1))
    row_spec = pl.BlockSpec((1, PEER_HEADS, SUBLANES, EXP_TM), lambda t, e: (e // per_blk, 0, 0, t))
    return pl.pallas_call(
        _expert_kernel,
        grid=(n_tok // EXP_TM, N_EXPERTS // EXP_TE),
        in_specs=[
            pl.BlockSpec((EXP_TM, d), lambda t, e: (t, 0), pipeline_mode=pl.Buffered(1)),
            pl.BlockSpec((EXP_TE, d), lambda t, e: (e, 0)),
            pl.BlockSpec((EXP_TE, d), lambda t, e: (e, 0)),
            lane_spec, lane_spec, row_spec, row_spec,
        ],
        out_specs=pl.BlockSpec((EXP_TM, d), lambda t, e: (t, 0)),
        out_shape=jax.ShapeDtypeStruct((n_tok, d), F32),
        scratch_shapes=[pltpu.VMEM((EXP_TE, EXP_TM), F32), pltpu.VMEM((EXP_TE, EXP_TM), BF16)],
        compiler_params=pltpu.CompilerParams(
            dimension_semantics=("arbitrary", "arbitrary"), vmem_limit_bytes=58 * 2**20),
        name="peer_experts",
    )(h2, u, v, r2, e2, c, w1)


def _final_kernel(x_ref, acc_ref, mod_ref, o_ref):
    o_ref[0] = x_ref[0] + mod_ref[0, 5:6, :] * acc_ref[0]


def _final_call(x1, acc, mod):
    batch, seq, d = x1.shape
    tok_spec = pl.BlockSpec((1, FINAL_BLOCK, d), lambda b, i: (b, i, 0))
    return pl.pallas_call(
        _final_kernel,
        grid=(batch, seq // FINAL_BLOCK),
        in_specs=[tok_spec, tok_spec, pl.BlockSpec((1, 6, d), lambda b, i: (b, 0, 0))],
        out_specs=tok_spec,
        out_shape=jax.ShapeDtypeStruct((batch, seq, d), F32),
        compiler_params=pltpu.CompilerParams(dimension_semantics=("arbitrary", "arbitrary")),
        name="final_residual",
    )(x1, acc, mod)


def kernel(x, c, rel_bias, ada_w, ada_b, norm1_gain, w_in, q_gain, k_gain, sinks, w_pool, pool_scale, w_out,
           norm2_gain, peer_w_query, peer_sub_keys, peer_u, peer_v):
    depth = ada_w.shape[0]
    batch, seq, d = x.shape
    mod = _ada_call(c, ada_w, ada_b).reshape(depth, batch, 6, d)
    bias = _bias_call(rel_bias)
    acc = None
    for l in range(depth):
        x = _mix_call(
            x, acc, mod[l - 1] if l > 0 else None, mod[l], norm1_gain[l].reshape(1, d), w_in[l].astype(BF16),
            q_gain[l].reshape(1, HEAD_DIM), k_gain[l].reshape(1, HEAD_DIM), sinks[l], bias,
            w_pool[l].astype(BF16), pool_scale[l].reshape(1, D_POOL), w_out[l].astype(BF16))
        h2, r2, e2, cnt, w1 = _route_call(
            x, mod[l], norm2_gain[l].reshape(1, d), peer_w_query[l].astype(BF16), peer_sub_keys[l].astype(BF16))
        acc = _expert_call(h2.reshape(batch * seq, d), peer_u[l], peer_v[l], r2, e2, cnt, w1)
        acc = acc.reshape(batch, seq, d)
    return _final_call(x, acc, mod[depth - 1])
```

```python
import functools
import math

import jax
import jax.numpy as jnp
import numpy as np
from jax import lax
from jax.experimental import pallas as pl
from jax.experimental.pallas import tpu as pltpu

F32 = jnp.float32
BF16 = jnp.bfloat16

D_MODEL = 2048
CHUNK = 64
D_POOL = 1024
POOL_WINDOWS = (2, 4, 8, 16)
POOL_GROUP = D_POOL // len(POOL_WINDOWS)
D_ATTN = 1024
HEAD_DIM = 64
N_Q_HEADS = 16
N_KV_HEADS = 2
Q_PER_KV = N_Q_HEADS // N_KV_HEADS
WINDOW = 128
N_BUCKETS = 32
MAX_DISTANCE = 128
D_KV = N_KV_HEADS * HEAD_DIM
D_IN = D_POOL + D_ATTN + 2 * D_KV
PEER_HEADS = 8
N_KEYS = 128
N_EXPERTS = N_KEYS * N_KEYS
PEER_TOPK = 16
D_QUERY = 256
D_HALF = D_QUERY // 2
EPS = 1e-6

LANES = 128
SUBLANES = 8

ADA_TN = 1024
Q_BLOCK = 256
K_BLOCK = Q_BLOCK + WINDOW
POOL_HALO = 16
ROUTE_BLOCK = 256
EXP_TM = 1024
EXP_TE = 512
I1_PER_TILE = EXP_TE // N_KEYS
BF16_ROWS = 2 * SUBLANES
FINAL_BLOCK = 512

NEG_INF = float("-inf")


def _rms(xf, gain):
    return xf * lax.rsqrt(jnp.mean(xf * xf, axis=-1, keepdims=True) + EPS) * gain


def _ada_kernel(ct_ref, w_ref, b_ref, o_ref):
    w = w_ref[0]
    n_batch = ct_ref.shape[1]
    for b in range(n_batch):
        cb = ct_ref[:, b:b + 1]
        cb = cb * jax.nn.sigmoid(cb)
        o_ref[0, b:b + 1, :] = jnp.sum(w * cb, axis=0, keepdims=True) + b_ref[0]


def _ada_call(c, ada_w, ada_b):
    depth, d, n = ada_w.shape
    batch = c.shape[0]
    return pl.pallas_call(
        _ada_kernel,
        grid=(depth, n // ADA_TN),
        in_specs=[
            pl.BlockSpec((d, batch), lambda l, j: (0, 0)),
            pl.BlockSpec((1, d, ADA_TN), lambda l, j: (l, 0, j)),
            pl.BlockSpec((1, 1, ADA_TN), lambda l, j: (l, 0, j)),
        ],
        out_specs=pl.BlockSpec((1, batch, ADA_TN), lambda l, j: (l, 0, j)),
        out_shape=jax.ShapeDtypeStruct((depth, batch, n), F32),
        compiler_params=pltpu.CompilerParams(
            dimension_semantics=("arbitrary", "arbitrary"), vmem_limit_bytes=40 * 2**20),
        name="ada_mod",
    )(c.T, ada_w, ada_b.reshape(depth, 1, n))


def _t5_bucket(rel):
    nb = N_BUCKETS // 2
    max_exact = nb // 2
    ret = jnp.where(rel > 0, nb, 0)
    n = jnp.abs(rel)
    nf = jnp.maximum(n, 1).astype(jnp.float32)
    large = max_exact + (jnp.log(nf / max_exact) / math.log(MAX_DISTANCE / max_exact) * (nb - max_exact)).astype(jnp.int32)
    large = jnp.minimum(large, nb - 1)
    return ret + jnp.where(n < max_exact, n, large)


def _bias_kernel(rb_ref, bucket_ref, valid_ref, o_ref):
    h = pl.program_id(0)
    bucket = bucket_ref[...]
    acc = jnp.zeros(bucket.shape, F32)
    for b in range(N_BUCKETS):
        acc = jnp.where(bucket == b, rb_ref[b, h], acc)
    o_ref[0] = jnp.where(valid_ref[...] > 0, acc, NEG_INF)


def _bias_call(rel_bias):
    a = jnp.arange(Q_BLOCK)[:, None]
    bk = jnp.arange(K_BLOCK)[None, :]
    rel = bk - WINDOW - a
    bucket = _t5_bucket(rel).astype(jnp.int32)
    dchunk = (bk - WINDOW) // CHUNK - a // CHUNK
    valid = ((dchunk >= -(WINDOW // CHUNK)) & (dchunk <= 0)).astype(jnp.int32)
    return pl.pallas_call(
        _bias_kernel,
        grid=(N_Q_HEADS,),
        in_specs=[
            pl.BlockSpec(memory_space=pltpu.SMEM),
            pl.BlockSpec((Q_BLOCK, K_BLOCK), lambda h: (0, 0)),
            pl.BlockSpec((Q_BLOCK, K_BLOCK), lambda h: (0, 0)),
        ],
        out_specs=pl.BlockSpec((1, Q_BLOCK, K_BLOCK), lambda h: (h, 0, 0)),
        out_shape=jax.ShapeDtypeStruct((N_Q_HEADS, Q_BLOCK, K_BLOCK), F32),
        compiler_params=pltpu.CompilerParams(dimension_semantics=("arbitrary",)),
        name="rel_bias_tile",
    )(rel_bias.astype(F32), bucket, valid)


def _mix_kernel(has_acc, *refs):
    if has_acc:
        (x_ref, acc_ref, modp_ref, mod_ref, n1g_ref, w_in_ref, qg_ref, kg_ref, sinks_ref, bias_ref,
         w_pool_ref, ps_ref, w_out_ref, o_ref, p_ext, k_ext, v_ext, y_scr) = refs
    else:
        (x_ref, mod_ref, n1g_ref, w_in_ref, qg_ref, kg_ref, sinks_ref, bias_ref,
         w_pool_ref, ps_ref, w_out_ref, o_ref, p_ext, k_ext, v_ext, y_scr) = refs
    i = pl.program_id(1)
    x = x_ref[0]
    if has_acc:
        x = x + modp_ref[0, 5:6, :] * acc_ref[0]
    sh1 = mod_ref[0, 0:1, :]
    sc1 = mod_ref[0, 1:2, :]
    g1 = mod_ref[0, 2:3, :]
    h = _rms(x, n1g_ref[...]) * (1.0 + sc1) + sh1
    z = jnp.dot(h.astype(BF16), w_in_ref[...], preferred_element_type=F32)

    @pl.when(i == 0)
    def _():
        p_ext[0:POOL_HALO, :] = jnp.zeros((POOL_HALO, D_POOL), F32)
        k_ext[0:WINDOW, :] = jnp.zeros((WINDOW, D_KV), F32)
        v_ext[0:WINDOW, :] = jnp.zeros((WINDOW, D_KV), F32)

    @pl.when(i > 0)
    def _():
        p_ext[0:POOL_HALO, :] = p_ext[Q_BLOCK:Q_BLOCK + POOL_HALO, :]
        k_ext[0:WINDOW, :] = k_ext[Q_BLOCK:Q_BLOCK + WINDOW, :]
        v_ext[0:WINDOW, :] = v_ext[Q_BLOCK:Q_BLOCK + WINDOW, :]

    p_ext[POOL_HALO:POOL_HALO + Q_BLOCK, :] = z[:, 0:D_POOL]
    kg = kg_ref[...]
    for g in range(N_KV_HEADS):
        lo = D_POOL + D_ATTN + g * HEAD_DIM
        k_ext[WINDOW:WINDOW + Q_BLOCK, g * HEAD_DIM:(g + 1) * HEAD_DIM] = _rms(z[:, lo:lo + HEAD_DIM], kg)
    v_ext[WINDOW:WINDOW + Q_BLOCK, :] = z[:, D_POOL + D_ATTN + D_KV:D_IN]

    tpos = i * Q_BLOCK + lax.broadcasted_iota(jnp.int32, (Q_BLOCK, 1), 0)
    for g, w in enumerate(POOL_WINDOWS):
        c0 = g * POOL_GROUP
        cur = p_ext[POOL_HALO:POOL_HALO + Q_BLOCK, c0:c0 + POOL_GROUP]
        acc = cur
        for d in range(1, w):
            acc = acc + p_ext[POOL_HALO - d:POOL_HALO - d + Q_BLOCK, c0:c0 + POOL_GROUP]
        cnt = jnp.minimum(tpos + 1, w).astype(F32)
        pooled = acc / cnt - cur
        y = jnp.dot(pooled.astype(BF16), w_pool_ref[g], preferred_element_type=F32)
        y_scr[:, c0:c0 + POOL_GROUP] = y * ps_ref[:, c0:c0 + POOL_GROUP]

    qg = qg_ref[...]
    first_cols = lax.broadcasted_iota(jnp.int32, (Q_BLOCK, K_BLOCK), 1) < WINDOW
    hide = jnp.logical_and(first_cols, i == 0)
    for g in range(N_KV_HEADS):
        kn = k_ext[:, g * HEAD_DIM:(g + 1) * HEAD_DIM].astype(BF16)
        vv = v_ext[:, g * HEAD_DIM:(g + 1) * HEAD_DIM].astype(BF16)
        for hq in range(g * Q_PER_KV, (g + 1) * Q_PER_KV):
            lo = D_POOL + hq * HEAD_DIM
            qh = _rms(z[:, lo:lo + HEAD_DIM], qg).astype(BF16)
            s = lax.dot_general(qh, kn, (((1,), (1,)), ((), ())), preferred_element_type=F32)
            s = s * (1.0 / math.sqrt(HEAD_DIM)) + bias_ref[hq]
            s = jnp.where(hide, NEG_INF, s)
            sink = sinks_ref[hq]
            m = jnp.maximum(jnp.max(s, axis=-1, keepdims=True), sink)
            e = jnp.exp(s - m)
            den = jnp.sum(e, axis=-1, keepdims=True) + jnp.exp(sink - m)
            o = jnp.dot(e.astype(BF16), vv, preferred_element_type=F32) / den
            y_scr[:, D_POOL + hq * HEAD_DIM:D_POOL + (hq + 1) * HEAD_DIM] = o

    out = jnp.dot(y_scr[...].astype(BF16), w_out_ref[...], preferred_element_type=F32)
    o_ref[0] = x + g1 * out


def _const_spec(shape):
    nd = len(shape)
    return pl.BlockSpec(shape, lambda *_: (0,) * nd, pipeline_mode=pl.Buffered(1))


def _mix_call(x, acc, mod_prev, mod, n1g, w_in, qg, kg, sinks, bias, w_pool, ps, w_out):
    batch, seq, d = x.shape
    has_acc = acc is not None
    tok_spec = pl.BlockSpec((1, Q_BLOCK, d), lambda b, i: (b, i, 0))
    mod_spec = pl.BlockSpec((1, 6, d), lambda b, i: (b, 0, 0))
    in_specs = [tok_spec]
    args = [x]
    if has_acc:
        in_specs += [tok_spec, mod_spec]
        args += [acc, mod_prev]
    in_specs += [
        mod_spec,
        _const_spec((1, d)),
        _const_spec((d, D_IN)),
        _const_spec((1, HEAD_DIM)),
        _const_spec((1, HEAD_DIM)),
        pl.BlockSpec(memory_space=pltpu.SMEM),
        _const_spec((N_Q_HEADS, Q_BLOCK, K_BLOCK)),
        _const_spec((len(POOL_WINDOWS), POOL_GROUP, POOL_GROUP)),
        _const_spec((1, D_POOL)),
        _const_spec((d, d)),
    ]
    args += [mod, n1g, w_in, qg, kg, sinks, bias, w_pool, ps, w_out]
    return pl.pallas_call(
        functools.partial(_mix_kernel, has_acc),
        grid=(batch, seq // Q_BLOCK),
        in_specs=in_specs,
        out_specs=tok_spec,
        out_shape=jax.ShapeDtypeStruct((batch, seq, d), F32),
        scratch_shapes=[
            pltpu.VMEM((POOL_HALO + Q_BLOCK, D_POOL), F32),
            pltpu.VMEM((K_BLOCK, D_KV), F32),
            pltpu.VMEM((K_BLOCK, D_KV), F32),
            pltpu.VMEM((Q_BLOCK, d), F32),
        ],
        compiler_params=pltpu.CompilerParams(
            dimension_semantics=("arbitrary", "arbitrary"), vmem_limit_bytes=56 * 2**20),
        name="mixer",
    )(*args)


def _top16_rows(s):
    rows = lax.broadcasted_iota(jnp.int32, s.shape, 0).astype(F32)
    rank = jnp.full(s.shape, float(PEER_TOPK), F32)
    cur = s
    vals = []
    for r in range(PEER_TOPK):
        m = jnp.max(cur, axis=0, keepdims=True)
        first = jnp.min(jnp.where(cur == m, rows, float(N_KEYS)), axis=0, keepdims=True)
        sel = rows == first
        rank = jnp.where(sel, float(r), rank)
        cur = jnp.where(sel, NEG_INF, cur)
        vals.append(m)
    return rank, vals


def _stack16(vals):
    t = vals[0].shape[1]
    rows = lax.broadcasted_iota(jnp.int32, (PEER_TOPK, t), 0)
    out = jnp.zeros((PEER_TOPK, t), F32)
    for j, v in enumerate(vals):
        out = jnp.where(rows == j, v, out)
    return out


def _select_counts(xs, ys):
    t = xs[0].shape[1]
    rows = lax.broadcasted_iota(jnp.int32, (PEER_TOPK, t), 0)
    rows_f = rows.astype(F32)
    x16 = _stack16(xs)
    y16 = _stack16(ys)
    cur, flat = [], []
    n_a = 4
    big = float(PEER_TOPK * PEER_TOPK)
    for k in range(n_a):
        ok = rows < PEER_TOPK // (k + 1)
        cur.append(jnp.where(ok, x16 + ys[k], NEG_INF))
        flat.append(jnp.where(ok, rows_f * float(PEER_TOPK) + float(k), big + 1.0))
    for j in range(3):
        ok = jnp.logical_and(rows >= n_a, rows < PEER_TOPK // (j + 1))
        cur.append(jnp.where(ok, y16 + xs[j], NEG_INF))
        flat.append(jnp.where(ok, rows_f + float(PEER_TOPK * j), big + 1.0))
    taken = [jnp.zeros((PEER_TOPK, t), F32) for _ in cur]
    top = xs[0] + ys[0]
    z = jnp.zeros((1, t), F32)
    for _ in range(PEER_TOPK):
        m = cur[0]
        for c in cur[1:]:
            m = jnp.maximum(m, c)
        m = jnp.max(m, axis=0, keepdims=True)
        f = jnp.where(cur[0] == m, flat[0], big)
        for c, fl in zip(cur[1:], flat[1:]):
            f = jnp.minimum(f, jnp.where(c == m, fl, big))
        f = jnp.min(f, axis=0, keepdims=True)
        for n in range(len(cur)):
            sel = flat[n] == f
            cur[n] = jnp.where(sel, NEG_INF, cur[n])
            taken[n] = jnp.where(sel, 1.0, taken[n])
        z = z + jnp.exp(m - top)
    cnt = taken[0]
    for k in range(1, n_a):
        cnt = cnt + taken[k]
    for j in range(3):
        cnt = cnt + jnp.where(rows == j, jnp.sum(taken[n_a + j], axis=0, keepdims=True), 0.0)
    return cnt, z


def _route_kernel(x_ref, mod_ref, n2g_ref, wq_ref, keys_ref, h2_ref, r2_ref, e2_ref, c_ref, w1_ref, q_scr):
    x = x_ref[0]
    sh2 = mod_ref[0, 3:4, :]
    sc2 = mod_ref[0, 4:5, :]
    h2 = (_rms(x, n2g_ref[...]) * (1.0 + sc2) + sh2).astype(BF16)
    h2_ref[0] = h2
    q_scr[...] = jnp.dot(h2, wq_ref[...], preferred_element_type=F32)
    k1 = keys_ref[0]
    k2 = keys_ref[1]
    t = q_scr.shape[0]

    def head(hd, carry):
        off = pl.multiple_of(hd * D_QUERY, D_QUERY)
        q1 = q_scr[:, pl.ds(off, D_HALF)].astype(BF16)
        q2 = q_scr[:, pl.ds(off + D_HALF, D_HALF)].astype(BF16)
        dn = (((1,), (1,)), ((), ()))
        s1 = lax.dot_general(k1, q1, dn, preferred_element_type=F32)
        s2 = lax.dot_general(k2, q2, dn, preferred_element_type=F32)
        r1, xs = _top16_rows(s1)
        r2, ys = _top16_rows(s2)
        cnt, z = _select_counts(xs, ys)
        c_full = jnp.zeros(s1.shape, F32)
        for j in range(PEER_TOPK):
            c_full = jnp.where(r1 == float(j), cnt[j:j + 1, :], c_full)
        w1 = jnp.exp(s1 - xs[0]) / z
        head_rows = pl.ds(pl.multiple_of(hd * N_KEYS, N_KEYS), N_KEYS)
        r2_ref[head_rows, :] = r2.astype(BF16)
        e2_ref[head_rows, :] = jnp.exp(s2 - ys[0]).astype(BF16)
        for blk in range(N_KEYS // I1_PER_TILE):
            rows = slice(blk * I1_PER_TILE, (blk + 1) * I1_PER_TILE)
            c_ref[blk, hd] = c_full[rows, :]
            w1_ref[blk, hd] = w1[rows, :]
        return carry

    lax.fori_loop(0, PEER_HEADS, head, 0)


def _route_call(x1, mod, n2g, wq, keys):
    batch, seq, d = x1.shape
    n_tok = batch * seq
    nblk = seq // ROUTE_BLOCK
    tok_spec = pl.BlockSpec((1, ROUTE_BLOCK, d), lambda b, i: (b, i, 0))
    lane_spec = pl.BlockSpec((PEER_HEADS * N_KEYS, ROUTE_BLOCK), lambda b, i: (0, b * nblk + i))
    row_spec = pl.BlockSpec((N_KEYS // I1_PER_TILE, PEER_HEADS, I1_PER_TILE, ROUTE_BLOCK),
                            lambda b, i: (0, 0, 0, b * nblk + i))
    lane_shape = jax.ShapeDtypeStruct((PEER_HEADS * N_KEYS, n_tok), BF16)
    row_shape = jax.ShapeDtypeStruct((N_KEYS // I1_PER_TILE, PEER_HEADS, I1_PER_TILE, n_tok), F32)
    return pl.pallas_call(
        _route_kernel,
        grid=(batch, nblk),
        in_specs=[
            tok_spec,
            pl.BlockSpec((1, 6, d), lambda b, i: (b, 0, 0)),
            _const_spec((1, d)),
            _const_spec((d, PEER_HEADS * D_QUERY)),
            _const_spec((2, N_KEYS, D_HALF)),
        ],
        out_specs=[tok_spec, lane_spec, lane_spec, row_spec, row_spec],
        out_shape=[jax.ShapeDtypeStruct((batch, seq, d), BF16), lane_shape, lane_shape, row_shape, row_shape],
        scratch_shapes=[pltpu.VMEM((ROUTE_BLOCK, PEER_HEADS * D_QUERY), F32)],
        compiler_params=pltpu.CompilerParams(
            dimension_semantics=("arbitrary", "arbitrary"), vmem_limit_bytes=48 * 2**20),
        name="peer_route",
    )(x1, mod, n2g, wq, keys)


def _expert_kernel(h2_ref, u_ref, v_ref, r2_in, e2_in, c_ref, w1_ref, o_ref, act_scr, g_scr, r2_ref, e2_ref):
    e = pl.program_id(1)

    @pl.when(e == 0)
    def _():
        r2_ref[...] = r2_in[...]
        e2_ref[...] = e2_in[...]

    u = u_ref[0].astype(BF16)
    act_scr[...] = lax.dot_general(u, h2_ref[...], (((1,), (1,)), ((), ())), preferred_element_type=F32)
    for lc in range(EXP_TM // LANES):
        ls = slice(lc * LANES, (lc + 1) * LANES)
        for a in range(I1_PER_TILE):
            cb = [jnp.broadcast_to(c_ref[0, hd, a:a + 1, ls], (BF16_ROWS, LANES)).astype(BF16)
                  for hd in range(PEER_HEADS)]
            wb = [jnp.broadcast_to(w1_ref[0, hd, a:a + 1, ls], (BF16_ROWS, LANES)).astype(BF16)
                  for hd in range(PEER_HEADS)]
            for c2 in range(N_KEYS // BF16_ROWS):
                gate = None
                for hd in range(PEER_HEADS):
                    rs = slice(hd * N_KEYS + c2 * BF16_ROWS, hd * N_KEYS + (c2 + 1) * BF16_ROWS)
                    term = jnp.where(r2_ref[rs, ls] < cb[hd], e2_ref[rs, ls] * wb[hd],
                                     jnp.zeros((), BF16))
                    gate = term if gate is None else gate + term
                es = slice(a * N_KEYS + c2 * BF16_ROWS, a * N_KEYS + (c2 + 1) * BF16_ROWS)
                act = act_scr[es, ls]
                gl = 0.5 * act * (1.0 + lax.erf(act * (1.0 / math.sqrt(2.0))))
                g_scr[es, ls] = (gl * gate.astype(F32)).astype(BF16)
    contrib = lax.dot_general(g_scr[...], v_ref[0].astype(BF16), (((0,), (0,)), ((), ())),
                              preferred_element_type=F32)

    @pl.when(e == 0)
    def _():
        o_ref[...] = contrib

    @pl.when(e > 0)
    def _():
        o_ref[...] += contrib


def _expert_call(layer, h2, u, v, r2, e2, c, w1):
    n_tok, d = h2.shape
    lane_spec = pl.BlockSpec((PEER_HEADS * N_KEYS, EXP_TM), lambda t, e: (0, t), pipeline_mode=pl.Buffered(1))
    row_spec = pl.BlockSpec((1, PEER_HEADS, I1_PER_TILE, EXP_TM), lambda t, e: (e, 0, 0, t))
    return pl.pallas_call(
        _expert_kernel,
        grid=(n_tok // EXP_TM, N_EXPERTS // EXP_TE),
        in_specs=[
            pl.BlockSpec((EXP_TM, d), lambda t, e: (t, 0), pipeline_mode=pl.Buffered(1)),
            pl.BlockSpec((1, EXP_TE, d), lambda t, e: (layer, e, 0)),
            pl.BlockSpec((1, EXP_TE, d), lambda t, e: (layer, e, 0)),
            lane_spec, lane_spec, row_spec, row_spec,
        ],
        out_specs=pl.BlockSpec((EXP_TM, d), lambda t, e: (t, 0)),
        out_shape=jax.ShapeDtypeStruct((n_tok, d), F32),
        scratch_shapes=[pltpu.VMEM((EXP_TE, EXP_TM), F32), pltpu.VMEM((EXP_TE, EXP_TM), BF16),
                        pltpu.VMEM((PEER_HEADS * N_KEYS, EXP_TM), BF16),
                        pltpu.VMEM((PEER_HEADS * N_KEYS, EXP_TM), BF16)],
        compiler_params=pltpu.CompilerParams(
            dimension_semantics=("arbitrary", "arbitrary"), vmem_limit_bytes=58 * 2**20),
        name="peer_experts",
    )(h2, u, v, r2, e2, c, w1)


def _final_kernel(x_ref, acc_ref, mod_ref, o_ref):
    o_ref[0] = x_ref[0] + mod_ref[0, 5:6, :] * acc_ref[0]


def _final_call(x1, acc, mod):
    batch, seq, d = x1.shape
    tok_spec = pl.BlockSpec((1, FINAL_BLOCK, d), lambda b, i: (b, i, 0))
    return pl.pallas_call(
        _final_kernel,
        grid=(batch, seq // FINAL_BLOCK),
        in_specs=[tok_spec, tok_spec, pl.BlockSpec((1, 6, d), lambda b, i: (b, 0, 0))],
        out_specs=tok_spec,
        out_shape=jax.ShapeDtypeStruct((batch, seq, d), F32),
        compiler_params=pltpu.CompilerParams(dimension_semantics=("arbitrary", "arbitrary")),
        name="final_residual",
    )(x1, acc, mod)


def kernel(x, c, rel_bias, ada_w, ada_b, norm1_gain, w_in, q_gain, k_gain, sinks, w_pool, pool_scale, w_out,
           norm2_gain, peer_w_query, peer_sub_keys, peer_u, peer_v):
    depth = ada_w.shape[0]
    batch, seq, d = x.shape
    mod = _ada_call(c, ada_w, ada_b).reshape(depth, batch, 6, d)
    bias = _bias_call(rel_bias)
    acc = None
    for l in range(depth):
        x = _mix_call(
            x, acc, mod[l - 1] if l > 0 else None, mod[l], norm1_gain[l].reshape(1, d), w_in[l].astype(BF16),
            q_gain[l].reshape(1, HEAD_DIM), k_gain[l].reshape(1, HEAD_DIM), sinks[l], bias,
            w_pool[l].astype(BF16), pool_scale[l].reshape(1, D_POOL), w_out[l].astype(BF16))
        h2, r2, e2, cnt, w1 = _route_call(
            x, mod[l], norm2_gain[l].reshape(1, d), peer_w_query[l].astype(BF16), peer_sub_keys[l].astype(BF16))
        acc = _expert_call(l, h2.reshape(batch * seq, d), peer_u, peer_v, r2, e2, cnt, w1)
        acc = acc.reshape(batch, seq, d)
    return _final_call(x, acc, mod[depth - 1])
```

```python
import functools
import math

import jax
import jax.numpy as jnp
import numpy as np
from jax import lax
from jax.experimental import pallas as pl
from jax.experimental.pallas import tpu as pltpu

F32 = jnp.float32
BF16 = jnp.bfloat16

D_MODEL = 2048
CHUNK = 64
D_POOL = 1024
POOL_WINDOWS = (2, 4, 8, 16)
POOL_GROUP = D_POOL // len(POOL_WINDOWS)
D_ATTN = 1024
HEAD_DIM = 64
N_Q_HEADS = 16
N_KV_HEADS = 2
Q_PER_KV = N_Q_HEADS // N_KV_HEADS
WINDOW = 128
N_BUCKETS = 32
MAX_DISTANCE = 128
D_KV = N_KV_HEADS * HEAD_DIM
D_IN = D_POOL + D_ATTN + 2 * D_KV
PEER_HEADS = 8
N_KEYS = 128
N_EXPERTS = N_KEYS * N_KEYS
PEER_TOPK = 16
D_QUERY = 256
D_HALF = D_QUERY // 2
EPS = 1e-6

LANES = 128
SUBLANES = 8

ADA_TN = 1024
Q_BLOCK = 256
K_BLOCK = Q_BLOCK + WINDOW
POOL_HALO = 16
ROUTE_BLOCK = 256
EXP_TM = 1024
EXP_TE = 512
I1_PER_TILE = EXP_TE // N_KEYS
BF16_ROWS = 2 * SUBLANES
FINAL_BLOCK = 512

NEG_INF = float("-inf")


def _rms(xf, gain):
    return xf * lax.rsqrt(jnp.mean(xf * xf, axis=-1, keepdims=True) + EPS) * gain


def _ada_kernel(ct_ref, w_ref, b_ref, o_ref):
    w = w_ref[0]
    n_batch = ct_ref.shape[1]
    for b in range(n_batch):
        cb = ct_ref[:, b:b + 1]
        cb = cb * jax.nn.sigmoid(cb)
        o_ref[0, b:b + 1, :] = jnp.sum(w * cb, axis=0, keepdims=True) + b_ref[0]


def _ada_call(c, ada_w, ada_b):
    depth, d, n = ada_w.shape
    batch = c.shape[0]
    return pl.pallas_call(
        _ada_kernel,
        grid=(depth, n // ADA_TN),
        in_specs=[
            pl.BlockSpec((d, batch), lambda l, j: (0, 0)),
            pl.BlockSpec((1, d, ADA_TN), lambda l, j: (l, 0, j)),
            pl.BlockSpec((1, 1, ADA_TN), lambda l, j: (l, 0, j)),
        ],
        out_specs=pl.BlockSpec((1, batch, ADA_TN), lambda l, j: (l, 0, j)),
        out_shape=jax.ShapeDtypeStruct((depth, batch, n), F32),
        compiler_params=pltpu.CompilerParams(
            dimension_semantics=("arbitrary", "arbitrary"), vmem_limit_bytes=40 * 2**20),
        name="ada_mod",
    )(c.T, ada_w, ada_b.reshape(depth, 1, n))


def _t5_bucket(rel):
    nb = N_BUCKETS // 2
    max_exact = nb // 2
    ret = jnp.where(rel > 0, nb, 0)
    n = jnp.abs(rel)
    nf = jnp.maximum(n, 1).astype(jnp.float32)
    large = max_exact + (jnp.log(nf / max_exact) / math.log(MAX_DISTANCE / max_exact) * (nb - max_exact)).astype(jnp.int32)
    large = jnp.minimum(large, nb - 1)
    return ret + jnp.where(n < max_exact, n, large)


def _bias_kernel(rb_ref, bucket_ref, valid_ref, o_ref):
    h = pl.program_id(0)
    bucket = bucket_ref[...]
    acc = jnp.zeros(bucket.shape, F32)
    for b in range(N_BUCKETS):
        acc = jnp.where(bucket == b, rb_ref[b, h], acc)
    o_ref[0] = jnp.where(valid_ref[...] > 0, acc, NEG_INF)


def _bias_call(rel_bias):
    a = jnp.arange(Q_BLOCK)[:, None]
    bk = jnp.arange(K_BLOCK)[None, :]
    rel = bk - WINDOW - a
    bucket = _t5_bucket(rel).astype(jnp.int32)
    dchunk = (bk - WINDOW) // CHUNK - a // CHUNK
    valid = ((dchunk >= -(WINDOW // CHUNK)) & (dchunk <= 0)).astype(jnp.int32)
    return pl.pallas_call(
        _bias_kernel,
        grid=(N_Q_HEADS,),
        in_specs=[
            pl.BlockSpec(memory_space=pltpu.SMEM),
            pl.BlockSpec((Q_BLOCK, K_BLOCK), lambda h: (0, 0)),
            pl.BlockSpec((Q_BLOCK, K_BLOCK), lambda h: (0, 0)),
        ],
        out_specs=pl.BlockSpec((1, Q_BLOCK, K_BLOCK), lambda h: (h, 0, 0)),
        out_shape=jax.ShapeDtypeStruct((N_Q_HEADS, Q_BLOCK, K_BLOCK), F32),
        compiler_params=pltpu.CompilerParams(dimension_semantics=("arbitrary",)),
        name="rel_bias_tile",
    )(rel_bias.astype(F32), bucket, valid)


def _mix_kernel(has_acc, *refs):
    if has_acc:
        (x_ref, acc_ref, modp_ref, mod_ref, n1g_ref, w_in_ref, qg_ref, kg_ref, sinks_ref, bias_ref,
         w_pool_ref, ps_ref, w_out_ref, o_ref, p_ext, k_ext, v_ext, y_scr) = refs
    else:
        (x_ref, mod_ref, n1g_ref, w_in_ref, qg_ref, kg_ref, sinks_ref, bias_ref,
         w_pool_ref, ps_ref, w_out_ref, o_ref, p_ext, k_ext, v_ext, y_scr) = refs
    i = pl.program_id(1)
    x = x_ref[0]
    if has_acc:
        x = x + modp_ref[0, 5:6, :] * acc_ref[0]
    sh1 = mod_ref[0, 0:1, :]
    sc1 = mod_ref[0, 1:2, :]
    g1 = mod_ref[0, 2:3, :]
    h = _rms(x, n1g_ref[...]) * (1.0 + sc1) + sh1
    z = jnp.dot(h.astype(BF16), w_in_ref[...], preferred_element_type=F32)

    @pl.when(i == 0)
    def _():
        p_ext[0:POOL_HALO, :] = jnp.zeros((POOL_HALO, D_POOL), F32)
        k_ext[0:WINDOW, :] = jnp.zeros((WINDOW, D_KV), F32)
        v_ext[0:WINDOW, :] = jnp.zeros((WINDOW, D_KV), F32)

    @pl.when(i > 0)
    def _():
        p_ext[0:POOL_HALO, :] = p_ext[Q_BLOCK:Q_BLOCK + POOL_HALO, :]
        k_ext[0:WINDOW, :] = k_ext[Q_BLOCK:Q_BLOCK + WINDOW, :]
        v_ext[0:WINDOW, :] = v_ext[Q_BLOCK:Q_BLOCK + WINDOW, :]

    p_ext[POOL_HALO:POOL_HALO + Q_BLOCK, :] = z[:, 0:D_POOL]
    kg = kg_ref[...]
    for g in range(N_KV_HEADS):
        lo = D_POOL + D_ATTN + g * HEAD_DIM
        k_ext[WINDOW:WINDOW + Q_BLOCK, g * HEAD_DIM:(g + 1) * HEAD_DIM] = _rms(z[:, lo:lo + HEAD_DIM], kg)
    v_ext[WINDOW:WINDOW + Q_BLOCK, :] = z[:, D_POOL + D_ATTN + D_KV:D_IN]

    tpos = i * Q_BLOCK + lax.broadcasted_iota(jnp.int32, (Q_BLOCK, 1), 0)
    for g, w in enumerate(POOL_WINDOWS):
        c0 = g * POOL_GROUP
        cur = p_ext[POOL_HALO:POOL_HALO + Q_BLOCK, c0:c0 + POOL_GROUP]
        acc = cur
        for d in range(1, w):
            acc = acc + p_ext[POOL_HALO - d:POOL_HALO - d + Q_BLOCK, c0:c0 + POOL_GROUP]
        cnt = jnp.minimum(tpos + 1, w).astype(F32)
        pooled = acc / cnt - cur
        y = jnp.dot(pooled.astype(BF16), w_pool_ref[g], preferred_element_type=F32)
        y_scr[:, c0:c0 + POOL_GROUP] = y * ps_ref[:, c0:c0 + POOL_GROUP]

    qg = qg_ref[...]
    first_cols = lax.broadcasted_iota(jnp.int32, (Q_BLOCK, K_BLOCK), 1) < WINDOW
    hide = jnp.logical_and(first_cols, i == 0)
    for g in range(N_KV_HEADS):
        kn = k_ext[:, g * HEAD_DIM:(g + 1) * HEAD_DIM].astype(BF16)
        vv = v_ext[:, g * HEAD_DIM:(g + 1) * HEAD_DIM].astype(BF16)
        for hq in range(g * Q_PER_KV, (g + 1) * Q_PER_KV):
            lo = D_POOL + hq * HEAD_DIM
            qh = _rms(z[:, lo:lo + HEAD_DIM], qg).astype(BF16)
            s = lax.dot_general(qh, kn, (((1,), (1,)), ((), ())), preferred_element_type=F32)
            s = s * (1.0 / math.sqrt(HEAD_DIM)) + bias_ref[hq]
            s = jnp.where(hide, NEG_INF, s)
            sink = sinks_ref[hq]
            m = jnp.maximum(jnp.max(s, axis=-1, keepdims=True), sink)
            e = jnp.exp(s - m)
            den = jnp.sum(e, axis=-1, keepdims=True) + jnp.exp(sink - m)
            o = jnp.dot(e.astype(BF16), vv, preferred_element_type=F32) / den
            y_scr[:, D_POOL + hq * HEAD_DIM:D_POOL + (hq + 1) * HEAD_DIM] = o

    out = jnp.dot(y_scr[...].astype(BF16), w_out_ref[...], preferred_element_type=F32)
    o_ref[0] = x + g1 * out


def _const_spec(shape):
    nd = len(shape)
    return pl.BlockSpec(shape, lambda *_: (0,) * nd, pipeline_mode=pl.Buffered(1))


def _mix_call(x, acc, mod_prev, mod, n1g, w_in, qg, kg, sinks, bias, w_pool, ps, w_out):
    batch, seq, d = x.shape
    has_acc = acc is not None
    tok_spec = pl.BlockSpec((1, Q_BLOCK, d), lambda b, i: (b, i, 0))
    mod_spec = pl.BlockSpec((1, 6, d), lambda b, i: (b, 0, 0))
    in_specs = [tok_spec]
    args = [x]
    if has_acc:
        in_specs += [tok_spec, mod_spec]
        args += [acc, mod_prev]
    in_specs += [
        mod_spec,
        _const_spec((1, d)),
        _const_spec((d, D_IN)),
        _const_spec((1, HEAD_DIM)),
        _const_spec((1, HEAD_DIM)),
        pl.BlockSpec(memory_space=pltpu.SMEM),
        _const_spec((N_Q_HEADS, Q_BLOCK, K_BLOCK)),
        _const_spec((len(POOL_WINDOWS), POOL_GROUP, POOL_GROUP)),
        _const_spec((1, D_POOL)),
        _const_spec((d, d)),
    ]
    args += [mod, n1g, w_in, qg, kg, sinks, bias, w_pool, ps, w_out]
    return pl.pallas_call(
        functools.partial(_mix_kernel, has_acc),
        grid=(batch, seq // Q_BLOCK),
        in_specs=in_specs,
        out_specs=tok_spec,
        out_shape=jax.ShapeDtypeStruct((batch, seq, d), F32),
        scratch_shapes=[
            pltpu.VMEM((POOL_HALO + Q_BLOCK, D_POOL), F32),
            pltpu.VMEM((K_BLOCK, D_KV), F32),
            pltpu.VMEM((K_BLOCK, D_KV), F32),
            pltpu.VMEM((Q_BLOCK, d), F32),
        ],
        compiler_params=pltpu.CompilerParams(
            dimension_semantics=("arbitrary", "arbitrary"), vmem_limit_bytes=56 * 2**20),
        name="mixer",
    )(*args)


def _top16_rows(s):
    rows = lax.broadcasted_iota(jnp.int32, s.shape, 0).astype(F32)
    rank = jnp.full(s.shape, float(PEER_TOPK), F32)
    cur = s
    vals = []
    for r in range(PEER_TOPK):
        m = jnp.max(cur, axis=0, keepdims=True)
        first = jnp.min(jnp.where(cur == m, rows, float(N_KEYS)), axis=0, keepdims=True)
        sel = rows == first
        rank = jnp.where(sel, float(r), rank)
        cur = jnp.where(sel, NEG_INF, cur)
        vals.append(m)
    return rank, vals


def _stack16(vals):
    t = vals[0].shape[1]
    rows = lax.broadcasted_iota(jnp.int32, (PEER_TOPK, t), 0)
    out = jnp.zeros((PEER_TOPK, t), F32)
    for j, v in enumerate(vals):
        out = jnp.where(rows == j, v, out)
    return out


def _select_counts(xs, ys):
    t = xs[0].shape[1]
    rows = lax.broadcasted_iota(jnp.int32, (PEER_TOPK, t), 0)
    rows_f = rows.astype(F32)
    x16 = _stack16(xs)
    y16 = _stack16(ys)
    cur, flat = [], []
    n_a = 4
    big = float(PEER_TOPK * PEER_TOPK)
    for k in range(n_a):
        ok = rows < PEER_TOPK // (k + 1)
        cur.append(jnp.where(ok, x16 + ys[k], NEG_INF))
        flat.append(jnp.where(ok, rows_f * float(PEER_TOPK) + float(k), big + 1.0))
    for j in range(3):
        ok = jnp.logical_and(rows >= n_a, rows < PEER_TOPK // (j + 1))
        cur.append(jnp.where(ok, y16 + xs[j], NEG_INF))
        flat.append(jnp.where(ok, rows_f + float(PEER_TOPK * j), big + 1.0))
    taken = [jnp.zeros((PEER_TOPK, t), F32) for _ in cur]
    top = xs[0] + ys[0]
    z = jnp.zeros((1, t), F32)
    for _ in range(PEER_TOPK):
        m = cur[0]
        for c in cur[1:]:
            m = jnp.maximum(m, c)
        m = jnp.max(m, axis=0, keepdims=True)
        f = jnp.where(cur[0] == m, flat[0], big)
        for c, fl in zip(cur[1:], flat[1:]):
            f = jnp.minimum(f, jnp.where(c == m, fl, big))
        f = jnp.min(f, axis=0, keepdims=True)
        for n in range(len(cur)):
            sel = flat[n] == f
            cur[n] = jnp.where(sel, NEG_INF, cur[n])
            taken[n] = jnp.where(sel, 1.0, taken[n])
        z = z + jnp.exp(m - top)
    cnt = taken[0]
    for k in range(1, n_a):
        cnt = cnt + taken[k]
    for j in range(3):
        cnt = cnt + jnp.where(rows == j, jnp.sum(taken[n_a + j], axis=0, keepdims=True), 0.0)
    return cnt, z


def _route_kernel(x_ref, mod_ref, n2g_ref, wq_ref, keys_ref, h2_ref, r2_ref, e2_ref, c_ref, w1_ref, q_scr):
    x = x_ref[0]
    sh2 = mod_ref[0, 3:4, :]
    sc2 = mod_ref[0, 4:5, :]
    h2 = (_rms(x, n2g_ref[...]) * (1.0 + sc2) + sh2).astype(BF16)
    h2_ref[0] = h2
    q_scr[...] = jnp.dot(h2, wq_ref[...], preferred_element_type=F32)
    k1 = keys_ref[0]
    k2 = keys_ref[1]
    t = q_scr.shape[0]

    def head(hd, carry):
        off = pl.multiple_of(hd * D_QUERY, D_QUERY)
        q1 = q_scr[:, pl.ds(off, D_HALF)].astype(BF16)
        q2 = q_scr[:, pl.ds(off + D_HALF, D_HALF)].astype(BF16)
        dn = (((1,), (1,)), ((), ()))
        s1 = lax.dot_general(k1, q1, dn, preferred_element_type=F32)
        s2 = lax.dot_general(k2, q2, dn, preferred_element_type=F32)
        r1, xs = _top16_rows(s1)
        r2, ys = _top16_rows(s2)
        cnt, z = _select_counts(xs, ys)
        c_full = jnp.zeros(s1.shape, F32)
        for j in range(PEER_TOPK):
            c_full = jnp.where(r1 == float(j), cnt[j:j + 1, :], c_full)
        w1 = jnp.exp(s1 - xs[0]) / z
        head_rows = pl.ds(pl.multiple_of(hd * N_KEYS, N_KEYS), N_KEYS)
        r2_ref[head_rows, :] = r2.astype(BF16)
        e2_ref[head_rows, :] = jnp.exp(s2 - ys[0]).astype(BF16)
        for blk in range(N_KEYS // I1_PER_TILE):
            rows = slice(blk * I1_PER_TILE, (blk + 1) * I1_PER_TILE)
            c_ref[blk, hd] = c_full[rows, :]
            w1_ref[blk, hd] = w1[rows, :]
        return carry

    lax.fori_loop(0, PEER_HEADS, head, 0)


def _route_call(x1, mod, n2g, wq, keys):
    batch, seq, d = x1.shape
    n_tok = batch * seq
    nblk = seq // ROUTE_BLOCK
    tok_spec = pl.BlockSpec((1, ROUTE_BLOCK, d), lambda b, i: (b, i, 0))
    lane_spec = pl.BlockSpec((PEER_HEADS * N_KEYS, ROUTE_BLOCK), lambda b, i: (0, b * nblk + i))
    row_spec = pl.BlockSpec((N_KEYS // I1_PER_TILE, PEER_HEADS, I1_PER_TILE, ROUTE_BLOCK),
                            lambda b, i: (0, 0, 0, b * nblk + i))
    lane_shape = jax.ShapeDtypeStruct((PEER_HEADS * N_KEYS, n_tok), BF16)
    row_shape = jax.ShapeDtypeStruct((N_KEYS // I1_PER_TILE, PEER_HEADS, I1_PER_TILE, n_tok), F32)
    return pl.pallas_call(
        _route_kernel,
        grid=(batch, nblk),
        in_specs=[
            tok_spec,
            pl.BlockSpec((1, 6, d), lambda b, i: (b, 0, 0)),
            _const_spec((1, d)),
            _const_spec((d, PEER_HEADS * D_QUERY)),
            _const_spec((2, N_KEYS, D_HALF)),
        ],
        out_specs=[tok_spec, lane_spec, lane_spec, row_spec, row_spec],
        out_shape=[jax.ShapeDtypeStruct((batch, seq, d), BF16), lane_shape, lane_shape, row_shape, row_shape],
        scratch_shapes=[pltpu.VMEM((ROUTE_BLOCK, PEER_HEADS * D_QUERY), F32)],
        compiler_params=pltpu.CompilerParams(
            dimension_semantics=("arbitrary", "arbitrary"), vmem_limit_bytes=48 * 2**20),
        name="peer_route",
    )(x1, mod, n2g, wq, keys)


def _expert_kernel(h2_ref, u_ref, v_ref, r2_in, e2_in, c_ref, w1_ref, o_ref, act_scr, g_scr, r2_ref, e2_ref):
    e = pl.program_id(1)

    @pl.when(e == 0)
    def _():
        r2_ref[...] = r2_in[...]
        e2_ref[...] = e2_in[...]

    u = u_ref[0].astype(BF16)
    act_scr[...] = lax.dot_general(u, h2_ref[...], (((1,), (1,)), ((), ())), preferred_element_type=F32)
    for lc in range(EXP_TM // LANES):
        ls = slice(lc * LANES, (lc + 1) * LANES)
        for a in range(I1_PER_TILE):
            cb = [jnp.broadcast_to(c_ref[0, hd, a:a + 1, ls], (BF16_ROWS, LANES)).astype(BF16)
                  for hd in range(PEER_HEADS)]
            wb = [jnp.broadcast_to(w1_ref[0, hd, a:a + 1, ls], (BF16_ROWS, LANES)).astype(BF16)
                  for hd in range(PEER_HEADS)]
            for c2 in range(N_KEYS // BF16_ROWS):
                gate = None
                for hd in range(PEER_HEADS):
                    rs = slice(hd * N_KEYS + c2 * BF16_ROWS, hd * N_KEYS + (c2 + 1) * BF16_ROWS)
                    term = jnp.where(r2_ref[rs, ls] < cb[hd], e2_ref[rs, ls] * wb[hd],
                                     jnp.zeros((), BF16))
                    gate = term if gate is None else gate + term
                es = slice(a * N_KEYS + c2 * BF16_ROWS, a * N_KEYS + (c2 + 1) * BF16_ROWS)
                act = act_scr[es, ls]
                gl = 0.5 * act * (1.0 + lax.erf(act * (1.0 / math.sqrt(2.0))))
                g_scr[es, ls] = (gl * gate.astype(F32)).astype(BF16)
    contrib = lax.dot_general(g_scr[...], v_ref[0].astype(BF16), (((0,), (0,)), ((), ())),
                              preferred_element_type=F32)

    @pl.when(e == 0)
    def _():
        o_ref[...] = contrib

    @pl.when(e > 0)
    def _():
        o_ref[...] += contrib


def _expert_call(layer, h2, u, v, r2, e2, c, w1):
    n_tok, d = h2.shape
    lane_spec = pl.BlockSpec((PEER_HEADS * N_KEYS, EXP_TM), lambda t, e: (0, t), pipeline_mode=pl.Buffered(1))
    row_spec = pl.BlockSpec((1, PEER_HEADS, I1_PER_TILE, EXP_TM), lambda t, e: (e, 0, 0, t))
    return pl.pallas_call(
        _expert_kernel,
        grid=(n_tok // EXP_TM, N_EXPERTS // EXP_TE),
        in_specs=[
            pl.BlockSpec((EXP_TM, d), lambda t, e: (t, 0), pipeline_mode=pl.Buffered(1)),
            pl.BlockSpec((1, EXP_TE, d), lambda t, e: (layer, e, 0)),
            pl.BlockSpec((1, EXP_TE, d), lambda t, e: (layer, e, 0)),
            lane_spec, lane_spec, row_spec, row_spec,
        ],
        out_specs=pl.BlockSpec((EXP_TM, d), lambda t, e: (t, 0)),
        out_shape=jax.ShapeDtypeStruct((n_tok, d), F32),
        scratch_shapes=[pltpu.VMEM((EXP_TE, EXP_TM), F32), pltpu.VMEM((EXP_TE, EXP_TM), BF16),
                        pltpu.VMEM((PEER_HEADS * N_KEYS, EXP_TM), BF16),
                        pltpu.VMEM((PEER_HEADS * N_KEYS, EXP_TM), BF16)],
        compiler_params=pltpu.CompilerParams(
            dimension_semantics=("arbitrary", "arbitrary"), vmem_limit_bytes=58 * 2**20),
        name="peer_experts",
    )(h2, u, v, r2, e2, c, w1)


def _final_kernel(x_ref, acc_ref, mod_ref, o_ref):
    o_ref[0] = x_ref[0] + mod_ref[0, 5:6, :] * acc_ref[0]


def _final_call(x1, acc, mod):
    batch, seq, d = x1.shape
    tok_spec = pl.BlockSpec((1, FINAL_BLOCK, d), lambda b, i: (b, i, 0))
    return pl.pallas_call(
        _final_kernel,
        grid=(batch, seq // FINAL_BLOCK),
        in_specs=[tok_spec, tok_spec, pl.BlockSpec((1, 6, d), lambda b, i: (b, 0, 0))],
        out_specs=tok_spec,
        out_shape=jax.ShapeDtypeStruct((batch, seq, d), F32),
        compiler_params=pltpu.CompilerParams(dimension_semantics=("arbitrary", "arbitrary")),
        name="final_residual",
    )(x1, acc, mod)


def kernel(x, c, rel_bias, ada_w, ada_b, norm1_gain, w_in, q_gain, k_gain, sinks, w_pool, pool_scale, w_out,
           norm2_gain, peer_w_query, peer_sub_keys, peer_u, peer_v):
    depth = ada_w.shape[0]
    batch, seq, d = x.shape
    mod = _ada_call(c, ada_w, ada_b).reshape(depth, batch, 6, d)
    bias = _bias_call(rel_bias)
    peer_u = peer_u.astype(BF16)
    peer_v = peer_v.astype(BF16)
    acc = None
    for l in range(depth):
        x = _mix_call(
            x, acc, mod[l - 1] if l > 0 else None, mod[l], norm1_gain[l].reshape(1, d), w_in[l].astype(BF16),
            q_gain[l].reshape(1, HEAD_DIM), k_gain[l].reshape(1, HEAD_DIM), sinks[l], bias,
            w_pool[l].astype(BF16), pool_scale[l].reshape(1, D_POOL), w_out[l].astype(BF16))
        h2, r2, e2, cnt, w1 = _route_call(
            x, mod[l], norm2_gain[l].reshape(1, d), peer_w_query[l].astype(BF16), peer_sub_keys[l].astype(BF16))
        acc = _expert_call(l, h2.reshape(batch * seq, d), peer_u, peer_v, r2, e2, cnt, w1)
        acc = acc.reshape(batch, seq, d)
    return _final_call(x, acc, mod[depth - 1])
```

```python
import functools
import math

import jax
import jax.numpy as jnp
import numpy as np
from jax import lax
from jax.experimental import pallas as pl
from jax.experimental.pallas import tpu as pltpu

F32 = jnp.float32
BF16 = jnp.bfloat16

D_MODEL = 2048
CHUNK = 64
D_POOL = 1024
POOL_WINDOWS = (2, 4, 8, 16)
POOL_GROUP = D_POOL // len(POOL_WINDOWS)
D_ATTN = 1024
HEAD_DIM = 64
N_Q_HEADS = 16
N_KV_HEADS = 2
Q_PER_KV = N_Q_HEADS // N_KV_HEADS
WINDOW = 128
N_BUCKETS = 32
MAX_DISTANCE = 128
D_KV = N_KV_HEADS * HEAD_DIM
D_IN = D_POOL + D_ATTN + 2 * D_KV
PEER_HEADS = 8
N_KEYS = 128
N_EXPERTS = N_KEYS * N_KEYS
PEER_TOPK = 16
D_QUERY = 256
D_HALF = D_QUERY // 2
EPS = 1e-6

LANES = 128
SUBLANES = 8

ADA_TN = 1024
Q_BLOCK = 256
K_BLOCK = Q_BLOCK + WINDOW
POOL_HALO = 16
ROUTE_BLOCK = 256
EXP_TM = 1024
EXP_TE = 512
I1_PER_TILE = EXP_TE // N_KEYS
BF16_ROWS = 2 * SUBLANES
PACKED_KEYS = N_KEYS // 2
EXP_CHUNKS = 4
FINAL_BLOCK = 512

NEG_INF = float("-inf")


def _rms(xf, gain):
    return xf * lax.rsqrt(jnp.mean(xf * xf, axis=-1, keepdims=True) + EPS) * gain


def _ada_kernel(ct_ref, w_ref, b_ref, o_ref):
    w = w_ref[0]
    n_batch = ct_ref.shape[1]
    for b in range(n_batch):
        cb = ct_ref[:, b:b + 1]
        cb = cb * jax.nn.sigmoid(cb)
        o_ref[0, b:b + 1, :] = jnp.sum(w * cb, axis=0, keepdims=True) + b_ref[0]


def _ada_call(c, ada_w, ada_b):
    depth, d, n = ada_w.shape
    batch = c.shape[0]
    return pl.pallas_call(
        _ada_kernel,
        grid=(depth, n // ADA_TN),
        in_specs=[
            pl.BlockSpec((d, batch), lambda l, j: (0, 0)),
            pl.BlockSpec((1, d, ADA_TN), lambda l, j: (l, 0, j)),
            pl.BlockSpec((1, 1, ADA_TN), lambda l, j: (l, 0, j)),
        ],
        out_specs=pl.BlockSpec((1, batch, ADA_TN), lambda l, j: (l, 0, j)),
        out_shape=jax.ShapeDtypeStruct((depth, batch, n), F32),
        compiler_params=pltpu.CompilerParams(
            dimension_semantics=("arbitrary", "arbitrary"), vmem_limit_bytes=40 * 2**20),
        name="ada_mod",
    )(c.T, ada_w, ada_b.reshape(depth, 1, n))


def _t5_bucket(rel):
    nb = N_BUCKETS // 2
    max_exact = nb // 2
    ret = jnp.where(rel > 0, nb, 0)
    n = jnp.abs(rel)
    nf = jnp.maximum(n, 1).astype(jnp.float32)
    large = max_exact + (jnp.log(nf / max_exact) / math.log(MAX_DISTANCE / max_exact) * (nb - max_exact)).astype(jnp.int32)
    large = jnp.minimum(large, nb - 1)
    return ret + jnp.where(n < max_exact, n, large)


def _bias_kernel(rb_ref, bucket_ref, valid_ref, o_ref):
    h = pl.program_id(0)
    bucket = bucket_ref[...]
    acc = jnp.zeros(bucket.shape, F32)
    for b in range(N_BUCKETS):
        acc = jnp.where(bucket == b, rb_ref[b, h], acc)
    o_ref[0] = jnp.where(valid_ref[...] > 0, acc, NEG_INF)


def _bias_call(rel_bias):
    a = jnp.arange(Q_BLOCK)[:, None]
    bk = jnp.arange(K_BLOCK)[None, :]
    rel = bk - WINDOW - a
    bucket = _t5_bucket(rel).astype(jnp.int32)
    dchunk = (bk - WINDOW) // CHUNK - a // CHUNK
    valid = ((dchunk >= -(WINDOW // CHUNK)) & (dchunk <= 0)).astype(jnp.int32)
    return pl.pallas_call(
        _bias_kernel,
        grid=(N_Q_HEADS,),
        in_specs=[
            pl.BlockSpec(memory_space=pltpu.SMEM),
            pl.BlockSpec((Q_BLOCK, K_BLOCK), lambda h: (0, 0)),
            pl.BlockSpec((Q_BLOCK, K_BLOCK), lambda h: (0, 0)),
        ],
        out_specs=pl.BlockSpec((1, Q_BLOCK, K_BLOCK), lambda h: (h, 0, 0)),
        out_shape=jax.ShapeDtypeStruct((N_Q_HEADS, Q_BLOCK, K_BLOCK), F32),
        compiler_params=pltpu.CompilerParams(dimension_semantics=("arbitrary",)),
        name="rel_bias_tile",
    )(rel_bias.astype(F32), bucket, valid)


def _mix_kernel(has_acc, *refs):
    if has_acc:
        (x_ref, acc_ref, modp_ref, mod_ref, n1g_ref, w_in_ref, qg_ref, kg_ref, sinks_ref, bias_ref,
         w_pool_ref, ps_ref, w_out_ref, o_ref, p_ext, k_ext, v_ext, y_scr) = refs
    else:
        (x_ref, mod_ref, n1g_ref, w_in_ref, qg_ref, kg_ref, sinks_ref, bias_ref,
         w_pool_ref, ps_ref, w_out_ref, o_ref, p_ext, k_ext, v_ext, y_scr) = refs
    i = pl.program_id(1)
    x = x_ref[0]
    if has_acc:
        x = x + modp_ref[0, 5:6, :] * acc_ref[0]
    sh1 = mod_ref[0, 0:1, :]
    sc1 = mod_ref[0, 1:2, :]
    g1 = mod_ref[0, 2:3, :]
    h = _rms(x, n1g_ref[...]) * (1.0 + sc1) + sh1
    z = jnp.dot(h.astype(BF16), w_in_ref[...], preferred_element_type=F32)

    @pl.when(i == 0)
    def _():
        p_ext[0:POOL_HALO, :] = jnp.zeros((POOL_HALO, D_POOL), F32)
        k_ext[0:WINDOW, :] = jnp.zeros((WINDOW, D_KV), F32)
        v_ext[0:WINDOW, :] = jnp.zeros((WINDOW, D_KV), F32)

    @pl.when(i > 0)
    def _():
        p_ext[0:POOL_HALO, :] = p_ext[Q_BLOCK:Q_BLOCK + POOL_HALO, :]
        k_ext[0:WINDOW, :] = k_ext[Q_BLOCK:Q_BLOCK + WINDOW, :]
        v_ext[0:WINDOW, :] = v_ext[Q_BLOCK:Q_BLOCK + WINDOW, :]

    p_ext[POOL_HALO:POOL_HALO + Q_BLOCK, :] = z[:, 0:D_POOL]
    kg = kg_ref[...]
    for g in range(N_KV_HEADS):
        lo = D_POOL + D_ATTN + g * HEAD_DIM
        k_ext[WINDOW:WINDOW + Q_BLOCK, g * HEAD_DIM:(g + 1) * HEAD_DIM] = _rms(z[:, lo:lo + HEAD_DIM], kg)
    v_ext[WINDOW:WINDOW + Q_BLOCK, :] = z[:, D_POOL + D_ATTN + D_KV:D_IN]

    tpos = i * Q_BLOCK + lax.broadcasted_iota(jnp.int32, (Q_BLOCK, 1), 0)
    for g, w in enumerate(POOL_WINDOWS):
        c0 = g * POOL_GROUP
        cur = p_ext[POOL_HALO:POOL_HALO + Q_BLOCK, c0:c0 + POOL_GROUP]
        acc = cur
        for d in range(1, w):
            acc = acc + p_ext[POOL_HALO - d:POOL_HALO - d + Q_BLOCK, c0:c0 + POOL_GROUP]
        cnt = jnp.minimum(tpos + 1, w).astype(F32)
        pooled = acc / cnt - cur
        y = jnp.dot(pooled.astype(BF16), w_pool_ref[g], preferred_element_type=F32)
        y_scr[:, c0:c0 + POOL_GROUP] = y * ps_ref[:, c0:c0 + POOL_GROUP]

    qg = qg_ref[...]
    first_cols = lax.broadcasted_iota(jnp.int32, (Q_BLOCK, K_BLOCK), 1) < WINDOW
    hide = jnp.logical_and(first_cols, i == 0)
    for g in range(N_KV_HEADS):
        kn = k_ext[:, g * HEAD_DIM:(g + 1) * HEAD_DIM].astype(BF16)
        vv = v_ext[:, g * HEAD_DIM:(g + 1) * HEAD_DIM].astype(BF16)
        for hq in range(g * Q_PER_KV, (g + 1) * Q_PER_KV):
            lo = D_POOL + hq * HEAD_DIM
            qh = _rms(z[:, lo:lo + HEAD_DIM], qg).astype(BF16)
            s = lax.dot_general(qh, kn, (((1,), (1,)), ((), ())), preferred_element_type=F32)
            s = s * (1.0 / math.sqrt(HEAD_DIM)) + bias_ref[hq]
            s = jnp.where(hide, NEG_INF, s)
            sink = sinks_ref[hq]
            m = jnp.maximum(jnp.max(s, axis=-1, keepdims=True), sink)
            e = jnp.exp(s - m)
            den = jnp.sum(e, axis=-1, keepdims=True) + jnp.exp(sink - m)
            o = jnp.dot(e.astype(BF16), vv, preferred_element_type=F32) / den
            y_scr[:, D_POOL + hq * HEAD_DIM:D_POOL + (hq + 1) * HEAD_DIM] = o

    out = jnp.dot(y_scr[...].astype(BF16), w_out_ref[...], preferred_element_type=F32)
    o_ref[0] = x + g1 * out


def _const_spec(shape):
    nd = len(shape)
    return pl.BlockSpec(shape, lambda *_: (0,) * nd, pipeline_mode=pl.Buffered(1))


def _mix_call(x, acc, mod_prev, mod, n1g, w_in, qg, kg, sinks, bias, w_pool, ps, w_out):
    batch, seq, d = x.shape
    has_acc = acc is not None
    tok_spec = pl.BlockSpec((1, Q_BLOCK, d), lambda b, i: (b, i, 0))
    mod_spec = pl.BlockSpec((1, 6, d), lambda b, i: (b, 0, 0))
    in_specs = [tok_spec]
    args = [x]
    if has_acc:
        in_specs += [tok_spec, mod_spec]
        args += [acc, mod_prev]
    in_specs += [
        mod_spec,
        _const_spec((1, d)),
        _const_spec((d, D_IN)),
        _const_spec((1, HEAD_DIM)),
        _const_spec((1, HEAD_DIM)),
        pl.BlockSpec(memory_space=pltpu.SMEM),
        _const_spec((N_Q_HEADS, Q_BLOCK, K_BLOCK)),
        _const_spec((len(POOL_WINDOWS), POOL_GROUP, POOL_GROUP)),
        _const_spec((1, D_POOL)),
        _const_spec((d, d)),
    ]
    args += [mod, n1g, w_in, qg, kg, sinks, bias, w_pool, ps, w_out]
    return pl.pallas_call(
        functools.partial(_mix_kernel, has_acc),
        grid=(batch, seq // Q_BLOCK),
        in_specs=in_specs,
        out_specs=tok_spec,
        out_shape=jax.ShapeDtypeStruct((batch, seq, d), F32),
        scratch_shapes=[
            pltpu.VMEM((POOL_HALO + Q_BLOCK, D_POOL), F32),
            pltpu.VMEM((K_BLOCK, D_KV), F32),
            pltpu.VMEM((K_BLOCK, D_KV), F32),
            pltpu.VMEM((Q_BLOCK, d), F32),
        ],
        compiler_params=pltpu.CompilerParams(
            dimension_semantics=("arbitrary", "arbitrary"), vmem_limit_bytes=56 * 2**20),
        name="mixer",
    )(*args)


def _top16_rows(s):
    rows = lax.broadcasted_iota(jnp.int32, s.shape, 0).astype(F32)
    rank = jnp.full(s.shape, float(PEER_TOPK), F32)
    cur = s
    vals = []
    for r in range(PEER_TOPK):
        m = jnp.max(cur, axis=0, keepdims=True)
        first = jnp.min(jnp.where(cur == m, rows, float(N_KEYS)), axis=0, keepdims=True)
        sel = rows == first
        rank = jnp.where(sel, float(r), rank)
        cur = jnp.where(sel, NEG_INF, cur)
        vals.append(m)
    return rank, vals


def _stack16(vals):
    t = vals[0].shape[1]
    rows = lax.broadcasted_iota(jnp.int32, (PEER_TOPK, t), 0)
    out = jnp.zeros((PEER_TOPK, t), F32)
    for j, v in enumerate(vals):
        out = jnp.where(rows == j, v, out)
    return out


def _select_counts(xs, ys):
    t = xs[0].shape[1]
    rows = lax.broadcasted_iota(jnp.int32, (PEER_TOPK, t), 0)
    rows_f = rows.astype(F32)
    x16 = _stack16(xs)
    y16 = _stack16(ys)
    cur, flat = [], []
    n_a = 4
    big = float(PEER_TOPK * PEER_TOPK)
    for k in range(n_a):
        ok = rows < PEER_TOPK // (k + 1)
        cur.append(jnp.where(ok, x16 + ys[k], NEG_INF))
        flat.append(jnp.where(ok, rows_f * float(PEER_TOPK) + float(k), big + 1.0))
    for j in range(3):
        ok = jnp.logical_and(rows >= n_a, rows < PEER_TOPK // (j + 1))
        cur.append(jnp.where(ok, y16 + xs[j], NEG_INF))
        flat.append(jnp.where(ok, rows_f + float(PEER_TOPK * j), big + 1.0))
    taken = [jnp.zeros((PEER_TOPK, t), F32) for _ in cur]
    top = xs[0] + ys[0]
    z = jnp.zeros((1, t), F32)
    for _ in range(PEER_TOPK):
        m = cur[0]
        for c in cur[1:]:
            m = jnp.maximum(m, c)
        m = jnp.max(m, axis=0, keepdims=True)
        f = jnp.where(cur[0] == m, flat[0], big)
        for c, fl in zip(cur[1:], flat[1:]):
            f = jnp.minimum(f, jnp.where(c == m, fl, big))
        f = jnp.min(f, axis=0, keepdims=True)
        for n in range(len(cur)):
            sel = flat[n] == f
            cur[n] = jnp.where(sel, NEG_INF, cur[n])
            taken[n] = jnp.where(sel, 1.0, taken[n])
        z = z + jnp.exp(m - top)
    cnt = taken[0]
    for k in range(1, n_a):
        cnt = cnt + taken[k]
    for j in range(3):
        cnt = cnt + jnp.where(rows == j, jnp.sum(taken[n_a + j], axis=0, keepdims=True), 0.0)
    return cnt, z


def _route_kernel(x_ref, mod_ref, n2g_ref, wq_ref, keys_ref, h2_ref, r2_ref, e2_ref, c_ref, w1_ref, q_scr):
    x = x_ref[0]
    sh2 = mod_ref[0, 3:4, :]
    sc2 = mod_ref[0, 4:5, :]
    h2 = (_rms(x, n2g_ref[...]) * (1.0 + sc2) + sh2).astype(BF16)
    h2_ref[0] = h2
    q_scr[...] = jnp.dot(h2, wq_ref[...], preferred_element_type=F32)
    k1 = keys_ref[0]
    k2 = keys_ref[1]
    t = q_scr.shape[0]

    def head(hd, carry):
        off = pl.multiple_of(hd * D_QUERY, D_QUERY)
        q1 = q_scr[:, pl.ds(off, D_HALF)].astype(BF16)
        q2 = q_scr[:, pl.ds(off + D_HALF, D_HALF)].astype(BF16)
        dn = (((1,), (1,)), ((), ()))
        s1 = lax.dot_general(k1, q1, dn, preferred_element_type=F32)
        s2 = lax.dot_general(k2, q2, dn, preferred_element_type=F32)
        r1, xs = _top16_rows(s1)
        r2, ys = _top16_rows(s2)
        cnt, z = _select_counts(xs, ys)
        c_full = jnp.zeros(s1.shape, F32)
        for j in range(PEER_TOPK):
            c_full = jnp.where(r1 == float(j), cnt[j:j + 1, :], c_full)
        w1 = jnp.exp(s1 - xs[0]) / z
        head_rows = pl.ds(pl.multiple_of(hd * PACKED_KEYS, PACKED_KEYS), PACKED_KEYS)
        r2_ref[head_rows, :] = pltpu.bitcast(r2.astype(BF16), jnp.uint32)
        e2_ref[head_rows, :] = pltpu.bitcast(jnp.exp(s2 - ys[0]).astype(BF16), jnp.uint32)
        for blk in range(N_KEYS // I1_PER_TILE):
            rows = slice(blk * I1_PER_TILE, (blk + 1) * I1_PER_TILE)
            c_ref[blk, hd] = c_full[rows, :]
            w1_ref[blk, hd] = w1[rows, :]
        return carry

    lax.fori_loop(0, PEER_HEADS, head, 0)


def _route_call(x1, mod, n2g, wq, keys):
    batch, seq, d = x1.shape
    n_tok = batch * seq
    nblk = seq // ROUTE_BLOCK
    tok_spec = pl.BlockSpec((1, ROUTE_BLOCK, d), lambda b, i: (b, i, 0))
    lane_spec = pl.BlockSpec((PEER_HEADS * PACKED_KEYS, ROUTE_BLOCK), lambda b, i: (0, b * nblk + i))
    row_spec = pl.BlockSpec((N_KEYS // I1_PER_TILE, PEER_HEADS, I1_PER_TILE, ROUTE_BLOCK),
                            lambda b, i: (0, 0, 0, b * nblk + i))
    lane_shape = jax.ShapeDtypeStruct((PEER_HEADS * PACKED_KEYS, n_tok), jnp.uint32)
    row_shape = jax.ShapeDtypeStruct((N_KEYS // I1_PER_TILE, PEER_HEADS, I1_PER_TILE, n_tok), F32)
    return pl.pallas_call(
        _route_kernel,
        grid=(batch, nblk),
        in_specs=[
            tok_spec,
            pl.BlockSpec((1, 6, d), lambda b, i: (b, 0, 0)),
            _const_spec((1, d)),
            _const_spec((d, PEER_HEADS * D_QUERY)),
            _const_spec((2, N_KEYS, D_HALF)),
        ],
        out_specs=[tok_spec, lane_spec, lane_spec, row_spec, row_spec],
        out_shape=[jax.ShapeDtypeStruct((batch, seq, d), BF16), lane_shape, lane_shape, row_shape, row_shape],
        scratch_shapes=[pltpu.VMEM((ROUTE_BLOCK, PEER_HEADS * D_QUERY), F32)],
        compiler_params=pltpu.CompilerParams(
            dimension_semantics=("arbitrary", "arbitrary"), vmem_limit_bytes=48 * 2**20),
        name="peer_route",
    )(x1, mod, n2g, wq, keys)


def _expert_step(h2_ref, u_ref, v_ref, r2_ref, e2_ref, c_ref, w1_ref, o_ref, act_w, act_r, g_w, g_r):
    u = u_ref[0].astype(BF16)
    v = v_ref[0].astype(BF16)
    gt = g_r[...].T
    tok = EXP_TM // EXP_CHUNKS
    col = D_MODEL // EXP_CHUNKS
    for n in range(EXP_CHUNKS):
        act_w[:, n * tok:(n + 1) * tok] = lax.dot_general(
            u, h2_ref[n * tok:(n + 1) * tok, :], (((1,), (1,)), ((), ())), preferred_element_type=F32)
        _gate_chunk(r2_ref, e2_ref, c_ref, w1_ref, act_r, g_w, n * tok // LANES, (n + 1) * tok // LANES)
        o_ref[:, n * col:(n + 1) * col] += jnp.dot(gt, v[:, n * col:(n + 1) * col], preferred_element_type=F32)


def _gate_chunk(r2_ref, e2_ref, c_ref, w1_ref, act_r, g_w, lc0, lc1):
    for lc in range(lc0, lc1):
        ls = slice(lc * LANES, (lc + 1) * LANES)
        for a in range(I1_PER_TILE):
            cb = [jnp.broadcast_to(c_ref[0, hd, a:a + 1, ls], (BF16_ROWS, LANES)).astype(BF16)
                  for hd in range(PEER_HEADS)]
            wb = [jnp.broadcast_to(w1_ref[0, hd, a:a + 1, ls], (BF16_ROWS, LANES)).astype(BF16)
                  for hd in range(PEER_HEADS)]
            for c2 in range(N_KEYS // BF16_ROWS):
                gate = None
                for hd in range(PEER_HEADS):
                    rs = slice(hd * PACKED_KEYS + c2 * SUBLANES, hd * PACKED_KEYS + (c2 + 1) * SUBLANES)
                    r2 = pltpu.bitcast(r2_ref[rs, ls], BF16)
                    e2 = pltpu.bitcast(e2_ref[rs, ls], BF16)
                    term = jnp.where(r2 < cb[hd], e2 * wb[hd], jnp.zeros((), BF16))
                    gate = term if gate is None else gate + term
                es = slice(a * N_KEYS + c2 * BF16_ROWS, a * N_KEYS + (c2 + 1) * BF16_ROWS)
                act = act_r[es, ls]
                gl = 0.5 * act * (1.0 + lax.erf(act * (1.0 / math.sqrt(2.0))))
                g_w[es, ls] = (gl * gate.astype(F32)).astype(BF16)


def _expert_kernel(n_items, n_exp_tiles, h2_ref, u_ref, v_ref, r2_ref, e2_ref, c_ref, w1_ref, o_ref,
                   act0, act1, g0, g1):
    s = pl.program_id(0)
    item_c = jnp.clip(s - 2, 0, n_items - 1)

    @pl.when(s == 0)
    def _():
        act1[...] = jnp.zeros(act1.shape, F32)
        g0[...] = jnp.zeros(g0.shape, BF16)

    @pl.when(item_c % n_exp_tiles == 0)
    def _():
        o_ref[...] = jnp.zeros(o_ref.shape, F32)

    io = (h2_ref, u_ref, v_ref, r2_ref, e2_ref, c_ref, w1_ref, o_ref)

    @pl.when(s % 2 == 0)
    def _():
        _expert_step(*io, act0, act1, g1, g0)

    @pl.when(s % 2 == 1)
    def _():
        _expert_step(*io, act1, act0, g0, g1)


def _expert_call(layer, h2, u, v, r2, e2, c, w1):
    n_tok, d = h2.shape
    n_exp_tiles = N_EXPERTS // EXP_TE
    n_items = (n_tok // EXP_TM) * n_exp_tiles

    def item(s, lag):
        it = jnp.clip(s - lag, 0, n_items - 1)
        return it // n_exp_tiles, it % n_exp_tiles

    lane_spec = pl.BlockSpec((PEER_HEADS * PACKED_KEYS, EXP_TM), lambda s: (0, item(s, 1)[0]),
                             pipeline_mode=pl.Buffered(1))
    row_spec = pl.BlockSpec((1, PEER_HEADS, I1_PER_TILE, EXP_TM),
                            lambda s: (item(s, 1)[1], 0, 0, item(s, 1)[0]))
    return pl.pallas_call(
        functools.partial(_expert_kernel, n_items, n_exp_tiles),
        grid=(n_items + 2,),
        in_specs=[
            pl.BlockSpec((EXP_TM, d), lambda s: (item(s, 0)[0], 0), pipeline_mode=pl.Buffered(1)),
            pl.BlockSpec((1, EXP_TE, d), lambda s: (layer, item(s, 0)[1], 0)),
            pl.BlockSpec((1, EXP_TE, d), lambda s: (layer, item(s, 2)[1], 0)),
            lane_spec, lane_spec, row_spec, row_spec,
        ],
        out_specs=pl.BlockSpec((EXP_TM, d), lambda s: (item(s, 2)[0], 0)),
        out_shape=jax.ShapeDtypeStruct((n_tok, d), F32),
        scratch_shapes=[pltpu.VMEM((EXP_TE, EXP_TM), F32), pltpu.VMEM((EXP_TE, EXP_TM), F32),
                        pltpu.VMEM((EXP_TE, EXP_TM), BF16), pltpu.VMEM((EXP_TE, EXP_TM), BF16)],
        compiler_params=pltpu.CompilerParams(
            dimension_semantics=("arbitrary",), vmem_limit_bytes=58 * 2**20),
        name="peer_experts",
    )(h2, u, v, r2, e2, c, w1)


def _final_kernel(x_ref, acc_ref, mod_ref, o_ref):
    o_ref[0] = x_ref[0] + mod_ref[0, 5:6, :] * acc_ref[0]


def _final_call(x1, acc, mod):
    batch, seq, d = x1.shape
    tok_spec = pl.BlockSpec((1, FINAL_BLOCK, d), lambda b, i: (b, i, 0))
    return pl.pallas_call(
        _final_kernel,
        grid=(batch, seq // FINAL_BLOCK),
        in_specs=[tok_spec, tok_spec, pl.BlockSpec((1, 6, d), lambda b, i: (b, 0, 0))],
        out_specs=tok_spec,
        out_shape=jax.ShapeDtypeStruct((batch, seq, d), F32),
        compiler_params=pltpu.CompilerParams(dimension_semantics=("arbitrary", "arbitrary")),
        name="final_residual",
    )(x1, acc, mod)


def kernel(x, c, rel_bias, ada_w, ada_b, norm1_gain, w_in, q_gain, k_gain, sinks, w_pool, pool_scale, w_out,
           norm2_gain, peer_w_query, peer_sub_keys, peer_u, peer_v):
    depth = ada_w.shape[0]
    batch, seq, d = x.shape
    mod = _ada_call(c, ada_w, ada_b).reshape(depth, batch, 6, d)
    bias = _bias_call(rel_bias)
    peer_u = peer_u.astype(BF16)
    peer_v = peer_v.astype(BF16)
    acc = None
    for l in range(depth):
        x = _mix_call(
            x, acc, mod[l - 1] if l > 0 else None, mod[l], norm1_gain[l].reshape(1, d), w_in[l].astype(BF16),
            q_gain[l].reshape(1, HEAD_DIM), k_gain[l].reshape(1, HEAD_DIM), sinks[l], bias,
            w_pool[l].astype(BF16), pool_scale[l].reshape(1, D_POOL), w_out[l].astype(BF16))
        h2, r2, e2, cnt, w1 = _route_call(
            x, mod[l], norm2_gain[l].reshape(1, d), peer_w_query[l].astype(BF16), peer_sub_keys[l].astype(BF16))
        acc = _expert_call(l, h2.reshape(batch * seq, d), peer_u, peer_v, r2, e2, cnt, w1)
        acc = acc.reshape(batch, seq, d)
    return _final_call(x, acc, mod[depth - 1])
```

```python
import functools
import math

import jax
import jax.numpy as jnp
import numpy as np
from jax import lax
from jax.experimental import pallas as pl
from jax.experimental.pallas import tpu as pltpu

F32 = jnp.float32
BF16 = jnp.bfloat16

D_MODEL = 2048
CHUNK = 64
D_POOL = 1024
POOL_WINDOWS = (2, 4, 8, 16)
POOL_GROUP = D_POOL // len(POOL_WINDOWS)
D_ATTN = 1024
HEAD_DIM = 64
N_Q_HEADS = 16
N_KV_HEADS = 2
Q_PER_KV = N_Q_HEADS // N_KV_HEADS
WINDOW = 128
N_BUCKETS = 32
MAX_DISTANCE = 128
D_KV = N_KV_HEADS * HEAD_DIM
D_IN = D_POOL + D_ATTN + 2 * D_KV
PEER_HEADS = 8
N_KEYS = 128
N_EXPERTS = N_KEYS * N_KEYS
PEER_TOPK = 16
D_QUERY = 256
D_HALF = D_QUERY // 2
EPS = 1e-6

LANES = 128
SUBLANES = 8

ADA_TN = 1024
Q_BLOCK = 256
K_BLOCK = Q_BLOCK + WINDOW
POOL_HALO = 16
ROUTE_BLOCK = 256
EXP_TM = 1024
EXP_TE = 512
I1_PER_TILE = EXP_TE // N_KEYS
BF16_ROWS = 2 * SUBLANES
PACKED_KEYS = N_KEYS // 2
EXP_KCHUNK = 256
FINAL_BLOCK = 512

NEG_INF = float("-inf")


def _rms(xf, gain):
    return xf * lax.rsqrt(jnp.mean(xf * xf, axis=-1, keepdims=True) + EPS) * gain


def _ada_kernel(ct_ref, w_ref, b_ref, o_ref):
    w = w_ref[0]
    n_batch = ct_ref.shape[1]
    for b in range(n_batch):
        cb = ct_ref[:, b:b + 1]
        cb = cb * jax.nn.sigmoid(cb)
        o_ref[0, b:b + 1, :] = jnp.sum(w * cb, axis=0, keepdims=True) + b_ref[0]


def _ada_call(c, ada_w, ada_b):
    depth, d, n = ada_w.shape
    batch = c.shape[0]
    return pl.pallas_call(
        _ada_kernel,
        grid=(depth, n // ADA_TN),
        in_specs=[
            pl.BlockSpec((d, batch), lambda l, j: (0, 0)),
            pl.BlockSpec((1, d, ADA_TN), lambda l, j: (l, 0, j)),
            pl.BlockSpec((1, 1, ADA_TN), lambda l, j: (l, 0, j)),
        ],
        out_specs=pl.BlockSpec((1, batch, ADA_TN), lambda l, j: (l, 0, j)),
        out_shape=jax.ShapeDtypeStruct((depth, batch, n), F32),
        compiler_params=pltpu.CompilerParams(
            dimension_semantics=("arbitrary", "arbitrary"), vmem_limit_bytes=40 * 2**20),
        name="ada_mod",
    )(c.T, ada_w, ada_b.reshape(depth, 1, n))


def _t5_bucket(rel):
    nb = N_BUCKETS // 2
    max_exact = nb // 2
    ret = jnp.where(rel > 0, nb, 0)
    n = jnp.abs(rel)
    nf = jnp.maximum(n, 1).astype(jnp.float32)
    large = max_exact + (jnp.log(nf / max_exact) / math.log(MAX_DISTANCE / max_exact) * (nb - max_exact)).astype(jnp.int32)
    large = jnp.minimum(large, nb - 1)
    return ret + jnp.where(n < max_exact, n, large)


def _bias_kernel(rb_ref, bucket_ref, valid_ref, o_ref):
    h = pl.program_id(0)
    bucket = bucket_ref[...]
    acc = jnp.zeros(bucket.shape, F32)
    for b in range(N_BUCKETS):
        acc = jnp.where(bucket == b, rb_ref[b, h], acc)
    o_ref[0] = jnp.where(valid_ref[...] > 0, acc, NEG_INF)


def _bias_call(rel_bias):
    a = jnp.arange(Q_BLOCK)[:, None]
    bk = jnp.arange(K_BLOCK)[None, :]
    rel = bk - WINDOW - a
    bucket = _t5_bucket(rel).astype(jnp.int32)
    dchunk = (bk - WINDOW) // CHUNK - a // CHUNK
    valid = ((dchunk >= -(WINDOW // CHUNK)) & (dchunk <= 0)).astype(jnp.int32)
    return pl.pallas_call(
        _bias_kernel,
        grid=(N_Q_HEADS,),
        in_specs=[
            pl.BlockSpec(memory_space=pltpu.SMEM),
            pl.BlockSpec((Q_BLOCK, K_BLOCK), lambda h: (0, 0)),
            pl.BlockSpec((Q_BLOCK, K_BLOCK), lambda h: (0, 0)),
        ],
        out_specs=pl.BlockSpec((1, Q_BLOCK, K_BLOCK), lambda h: (h, 0, 0)),
        out_shape=jax.ShapeDtypeStruct((N_Q_HEADS, Q_BLOCK, K_BLOCK), F32),
        compiler_params=pltpu.CompilerParams(dimension_semantics=("arbitrary",)),
        name="rel_bias_tile",
    )(rel_bias.astype(F32), bucket, valid)


def _mix_kernel(has_acc, *refs):
    if has_acc:
        (x_ref, acc_ref, modp_ref, mod_ref, n1g_ref, w_in_ref, qg_ref, kg_ref, sinks_ref, bias_ref,
         w_pool_ref, ps_ref, w_out_ref, o_ref, p_ext, k_ext, v_ext, y_scr) = refs
    else:
        (x_ref, mod_ref, n1g_ref, w_in_ref, qg_ref, kg_ref, sinks_ref, bias_ref,
         w_pool_ref, ps_ref, w_out_ref, o_ref, p_ext, k_ext, v_ext, y_scr) = refs
    i = pl.program_id(1)
    x = x_ref[0]
    if has_acc:
        x = x + modp_ref[0, 5:6, :] * acc_ref[0]
    sh1 = mod_ref[0, 0:1, :]
    sc1 = mod_ref[0, 1:2, :]
    g1 = mod_ref[0, 2:3, :]
    h = _rms(x, n1g_ref[...]) * (1.0 + sc1) + sh1
    z = jnp.dot(h.astype(BF16), w_in_ref[...], preferred_element_type=F32)

    @pl.when(i == 0)
    def _():
        p_ext[0:POOL_HALO, :] = jnp.zeros((POOL_HALO, D_POOL), F32)
        k_ext[0:WINDOW, :] = jnp.zeros((WINDOW, D_KV), F32)
        v_ext[0:WINDOW, :] = jnp.zeros((WINDOW, D_KV), F32)

    @pl.when(i > 0)
    def _():
        p_ext[0:POOL_HALO, :] = p_ext[Q_BLOCK:Q_BLOCK + POOL_HALO, :]
        k_ext[0:WINDOW, :] = k_ext[Q_BLOCK:Q_BLOCK + WINDOW, :]
        v_ext[0:WINDOW, :] = v_ext[Q_BLOCK:Q_BLOCK + WINDOW, :]

    p_ext[POOL_HALO:POOL_HALO + Q_BLOCK, :] = z[:, 0:D_POOL]
    kg = kg_ref[...]
    for g in range(N_KV_HEADS):
        lo = D_POOL + D_ATTN + g * HEAD_DIM
        k_ext[WINDOW:WINDOW + Q_BLOCK, g * HEAD_DIM:(g + 1) * HEAD_DIM] = _rms(z[:, lo:lo + HEAD_DIM], kg)
    v_ext[WINDOW:WINDOW + Q_BLOCK, :] = z[:, D_POOL + D_ATTN + D_KV:D_IN]

    tpos = i * Q_BLOCK + lax.broadcasted_iota(jnp.int32, (Q_BLOCK, 1), 0)
    for g, w in enumerate(POOL_WINDOWS):
        c0 = g * POOL_GROUP
        cur = p_ext[POOL_HALO:POOL_HALO + Q_BLOCK, c0:c0 + POOL_GROUP]
        acc = cur
        for d in range(1, w):
            acc = acc + p_ext[POOL_HALO - d:POOL_HALO - d + Q_BLOCK, c0:c0 + POOL_GROUP]
        cnt = jnp.minimum(tpos + 1, w).astype(F32)
        pooled = acc / cnt - cur
        y = jnp.dot(pooled.astype(BF16), w_pool_ref[g], preferred_element_type=F32)
        y_scr[:, c0:c0 + POOL_GROUP] = y * ps_ref[:, c0:c0 + POOL_GROUP]

    qg = qg_ref[...]
    first_cols = lax.broadcasted_iota(jnp.int32, (Q_BLOCK, K_BLOCK), 1) < WINDOW
    hide = jnp.logical_and(first_cols, i == 0)
    for g in range(N_KV_HEADS):
        kn = k_ext[:, g * HEAD_DIM:(g + 1) * HEAD_DIM].astype(BF16)
        vv = v_ext[:, g * HEAD_DIM:(g + 1) * HEAD_DIM].astype(BF16)
        for hq in range(g * Q_PER_KV, (g + 1) * Q_PER_KV):
            lo = D_POOL + hq * HEAD_DIM
            qh = _rms(z[:, lo:lo + HEAD_DIM], qg).astype(BF16)
            s = lax.dot_general(qh, kn, (((1,), (1,)), ((), ())), preferred_element_type=F32)
            s = s * (1.0 / math.sqrt(HEAD_DIM)) + bias_ref[hq]
            s = jnp.where(hide, NEG_INF, s)
            sink = sinks_ref[hq]
            m = jnp.maximum(jnp.max(s, axis=-1, keepdims=True), sink)
            e = jnp.exp(s - m)
            den = jnp.sum(e, axis=-1, keepdims=True) + jnp.exp(sink - m)
            o = jnp.dot(e.astype(BF16), vv, preferred_element_type=F32) / den
            y_scr[:, D_POOL + hq * HEAD_DIM:D_POOL + (hq + 1) * HEAD_DIM] = o

    out = jnp.dot(y_scr[...].astype(BF16), w_out_ref[...], preferred_element_type=F32)
    o_ref[0] = x + g1 * out


def _const_spec(shape):
    nd = len(shape)
    return pl.BlockSpec(shape, lambda *_: (0,) * nd, pipeline_mode=pl.Buffered(1))


def _mix_call(x, acc, mod_prev, mod, n1g, w_in, qg, kg, sinks, bias, w_pool, ps, w_out):
    batch, seq, d = x.shape
    has_acc = acc is not None
    tok_spec = pl.BlockSpec((1, Q_BLOCK, d), lambda b, i: (b, i, 0))
    mod_spec = pl.BlockSpec((1, 6, d), lambda b, i: (b, 0, 0))
    in_specs = [tok_spec]
    args = [x]
    if has_acc:
        in_specs += [tok_spec, mod_spec]
        args += [acc, mod_prev]
    in_specs += [
        mod_spec,
        _const_spec((1, d)),
        _const_spec((d, D_IN)),
        _const_spec((1, HEAD_DIM)),
        _const_spec((1, HEAD_DIM)),
        pl.BlockSpec(memory_space=pltpu.SMEM),
        _const_spec((N_Q_HEADS, Q_BLOCK, K_BLOCK)),
        _const_spec((len(POOL_WINDOWS), POOL_GROUP, POOL_GROUP)),
        _const_spec((1, D_POOL)),
        _const_spec((d, d)),
    ]
    args += [mod, n1g, w_in, qg, kg, sinks, bias, w_pool, ps, w_out]
    return pl.pallas_call(
        functools.partial(_mix_kernel, has_acc),
        grid=(batch, seq // Q_BLOCK),
        in_specs=in_specs,
        out_specs=tok_spec,
        out_shape=jax.ShapeDtypeStruct((batch, seq, d), F32),
        scratch_shapes=[
            pltpu.VMEM((POOL_HALO + Q_BLOCK, D_POOL), F32),
            pltpu.VMEM((K_BLOCK, D_KV), F32),
            pltpu.VMEM((K_BLOCK, D_KV), F32),
            pltpu.VMEM((Q_BLOCK, d), F32),
        ],
        compiler_params=pltpu.CompilerParams(
            dimension_semantics=("arbitrary", "arbitrary"), vmem_limit_bytes=56 * 2**20),
        name="mixer",
    )(*args)


def _top16_rows(s, exact):
    rows = lax.broadcasted_iota(jnp.int32, s.shape, 0).astype(F32)
    rank = jnp.full(s.shape, float(PEER_TOPK), F32)
    cur = s
    vals = []
    for r in range(PEER_TOPK):
        m = jnp.max(cur, axis=0, keepdims=True)
        sel = cur == m
        if exact:
            first = jnp.min(jnp.where(sel, rows, float(N_KEYS)), axis=0, keepdims=True)
            sel = rows == first
        rank = jnp.where(sel, float(r), rank)
        cur = jnp.where(sel, NEG_INF, cur)
        vals.append(m)
    return rank, vals


def _stack16(vals):
    t = vals[0].shape[1]
    rows = lax.broadcasted_iota(jnp.int32, (PEER_TOPK, t), 0)
    out = jnp.zeros((PEER_TOPK, t), F32)
    for j, v in enumerate(vals):
        out = jnp.where(rows == j, v, out)
    return out


def _select_counts(xs, ys, exact):
    t = xs[0].shape[1]
    rows = lax.broadcasted_iota(jnp.int32, (PEER_TOPK, t), 0)
    rows_f = rows.astype(F32)
    x16 = _stack16(xs)
    y16 = _stack16(ys)
    cur, flat = [], []
    n_a = 4
    big = float(PEER_TOPK * PEER_TOPK)
    for k in range(n_a):
        ok = rows < PEER_TOPK // (k + 1)
        cur.append(jnp.where(ok, x16 + ys[k], NEG_INF))
        flat.append(jnp.where(ok, rows_f * float(PEER_TOPK) + float(k), big + 1.0))
    for j in range(3):
        ok = jnp.logical_and(rows >= n_a, rows < PEER_TOPK // (j + 1))
        cur.append(jnp.where(ok, y16 + xs[j], NEG_INF))
        flat.append(jnp.where(ok, rows_f + float(PEER_TOPK * j), big + 1.0))
    taken = [jnp.zeros((PEER_TOPK, t), F32) for _ in cur]
    top = xs[0] + ys[0]
    z = jnp.zeros((1, t), F32)
    for _ in range(PEER_TOPK):
        m = cur[0]
        for c in cur[1:]:
            m = jnp.maximum(m, c)
        m = jnp.max(m, axis=0, keepdims=True)
        if exact:
            f = jnp.where(cur[0] == m, flat[0], big)
            for c, fl in zip(cur[1:], flat[1:]):
                f = jnp.minimum(f, jnp.where(c == m, fl, big))
            f = jnp.min(f, axis=0, keepdims=True)
        for n in range(len(cur)):
            sel = (flat[n] == f) if exact else (cur[n] == m)
            cur[n] = jnp.where(sel, NEG_INF, cur[n])
            taken[n] = jnp.where(sel, 1.0, taken[n])
        z = z + jnp.exp(m - top)
    cnt = taken[0]
    for k in range(1, n_a):
        cnt = cnt + taken[k]
    for j in range(3):
        cnt = cnt + jnp.where(rows == j, jnp.sum(taken[n_a + j], axis=0, keepdims=True), 0.0)
    return cnt, z


def _route_kernel(x_ref, mod_ref, n2g_ref, wq_ref, keys_ref, h2_ref, r2_ref, e2_ref, c_ref, w1_ref, q_scr):
    x = x_ref[0]
    sh2 = mod_ref[0, 3:4, :]
    sc2 = mod_ref[0, 4:5, :]
    h2 = (_rms(x, n2g_ref[...]) * (1.0 + sc2) + sh2).astype(BF16)
    h2_ref[0] = h2
    q_scr[...] = jnp.dot(h2, wq_ref[...], preferred_element_type=F32)
    k1 = keys_ref[0]
    k2 = keys_ref[1]
    t = q_scr.shape[0]

    def head(hd, carry):
        off = pl.multiple_of(hd * D_QUERY, D_QUERY)
        q1 = q_scr[:, pl.ds(off, D_HALF)].astype(BF16)
        q2 = q_scr[:, pl.ds(off + D_HALF, D_HALF)].astype(BF16)
        dn = (((1,), (1,)), ((), ()))
        s1 = lax.dot_general(k1, q1, dn, preferred_element_type=F32)
        s2 = lax.dot_general(k2, q2, dn, preferred_element_type=F32)
        def route(exact):
            r1, xs = _top16_rows(s1, exact)
            r2, ys = _top16_rows(s2, exact)
            cnt, z = _select_counts(xs, ys, exact)
            c_full = jnp.zeros(s1.shape, F32)
            for j in range(PEER_TOPK):
                c_full = jnp.where(r1 == float(j), cnt[j:j + 1, :], c_full)
            w1 = jnp.exp(s1 - xs[0]) / z
            head_rows = pl.ds(pl.multiple_of(hd * PACKED_KEYS, PACKED_KEYS), PACKED_KEYS)
            r2_ref[head_rows, :] = pltpu.bitcast(r2.astype(BF16), jnp.uint32)
            e2_ref[head_rows, :] = pltpu.bitcast(jnp.exp(s2 - ys[0]).astype(BF16), jnp.uint32)
            for blk in range(N_KEYS // I1_PER_TILE):
                rows = slice(blk * I1_PER_TILE, (blk + 1) * I1_PER_TILE)
                c_ref[blk, hd] = c_full[rows, :]
                w1_ref[blk, hd] = w1[rows, :]
            return r1, r2, cnt

        r1, r2, cnt = route(False)
        k16 = float(PEER_TOPK)
        n1 = jnp.sum(jnp.where(r1 < k16, 1.0, 0.0), axis=0, keepdims=True)
        n2 = jnp.sum(jnp.where(r2 < k16, 1.0, 0.0), axis=0, keepdims=True)
        nc = jnp.sum(cnt, axis=0, keepdims=True)
        ok = jnp.logical_and(jnp.logical_and(n1 == k16, n2 == k16), nc == k16)
        n_bad = jnp.sum(jnp.where(ok, 0.0, 1.0))

        @pl.when(n_bad > 0.0)
        def _():
            route(True)

        return carry

    lax.fori_loop(0, PEER_HEADS, head, 0)


def _route_call(x1, mod, n2g, wq, keys):
    batch, seq, d = x1.shape
    n_tok = batch * seq
    nblk = seq // ROUTE_BLOCK
    tok_spec = pl.BlockSpec((1, ROUTE_BLOCK, d), lambda b, i: (b, i, 0))
    lane_spec = pl.BlockSpec((PEER_HEADS * PACKED_KEYS, ROUTE_BLOCK), lambda b, i: (0, b * nblk + i))
    row_spec = pl.BlockSpec((N_KEYS // I1_PER_TILE, PEER_HEADS, I1_PER_TILE, ROUTE_BLOCK),
                            lambda b, i: (0, 0, 0, b * nblk + i))
    lane_shape = jax.ShapeDtypeStruct((PEER_HEADS * PACKED_KEYS, n_tok), jnp.uint32)
    row_shape = jax.ShapeDtypeStruct((N_KEYS // I1_PER_TILE, PEER_HEADS, I1_PER_TILE, n_tok), F32)
    return pl.pallas_call(
        _route_kernel,
        grid=(batch, nblk),
        in_specs=[
            tok_spec,
            pl.BlockSpec((1, 6, d), lambda b, i: (b, 0, 0)),
            _const_spec((1, d)),
            _const_spec((d, PEER_HEADS * D_QUERY)),
            _const_spec((2, N_KEYS, D_HALF)),
        ],
        out_specs=[tok_spec, lane_spec, lane_spec, row_spec, row_spec],
        out_shape=[jax.ShapeDtypeStruct((batch, seq, d), BF16), lane_shape, lane_shape, row_shape, row_shape],
        scratch_shapes=[pltpu.VMEM((ROUTE_BLOCK, PEER_HEADS * D_QUERY), F32)],
        compiler_params=pltpu.CompilerParams(
            dimension_semantics=("arbitrary", "arbitrary"), vmem_limit_bytes=48 * 2**20),
        name="peer_route",
    )(x1, mod, n2g, wq, keys)


def _expert_kernel(h2_ref, u_ref, v_ref, r2_ref, e2_ref, c_ref, w1_ref, o_ref, act_scr, g_scr):
    e = pl.program_id(1)

    @pl.when(e == 0)
    def _():
        o_ref[...] = jnp.zeros(o_ref.shape, F32)

    for k in range(EXP_TE // EXP_KCHUNK):
        rows = slice(k * EXP_KCHUNK, (k + 1) * EXP_KCHUNK)
        u = u_ref[0, rows, :].astype(BF16)
        act_scr[rows, :] = lax.dot_general(u, h2_ref[...], (((1,), (1,)), ((), ())),
                                           preferred_element_type=F32)
        _gate_chunk(r2_ref, e2_ref, c_ref, w1_ref, act_scr, g_scr,
                    k * EXP_KCHUNK // N_KEYS, (k + 1) * EXP_KCHUNK // N_KEYS)
        o_ref[...] += lax.dot_general(g_scr[rows, :], v_ref[0, rows, :].astype(BF16), (((0,), (0,)), ((), ())),
                                      preferred_element_type=F32)


def _gate_chunk(r2_ref, e2_ref, c_ref, w1_ref, act_r, g_w, a0, a1):
    for lc in range(EXP_TM // LANES):
        ls = slice(lc * LANES, (lc + 1) * LANES)
        for a in range(a0, a1):
            cb = [jnp.broadcast_to(c_ref[0, hd, a:a + 1, ls], (BF16_ROWS, LANES)).astype(BF16)
                  for hd in range(PEER_HEADS)]
            wb = [jnp.broadcast_to(w1_ref[0, hd, a:a + 1, ls], (BF16_ROWS, LANES)).astype(BF16)
                  for hd in range(PEER_HEADS)]
            for c2 in range(N_KEYS // BF16_ROWS):
                gate = None
                for hd in range(PEER_HEADS):
                    rs = slice(hd * PACKED_KEYS + c2 * SUBLANES, hd * PACKED_KEYS + (c2 + 1) * SUBLANES)
                    r2 = pltpu.bitcast(r2_ref[rs, ls], BF16)
                    e2 = pltpu.bitcast(e2_ref[rs, ls], BF16)
                    term = jnp.where(r2 < cb[hd], e2 * wb[hd], jnp.zeros((), BF16))
                    gate = term if gate is None else gate + term
                es = slice(a * N_KEYS + c2 * BF16_ROWS, a * N_KEYS + (c2 + 1) * BF16_ROWS)
                act = act_r[es, ls]
                gl = 0.5 * act * (1.0 + lax.erf(act * (1.0 / math.sqrt(2.0))))
                g_w[es, ls] = (gl * gate.astype(F32)).astype(BF16)


def _expert_call(layer, h2, u, v, r2, e2, c, w1):
    n_tok, d = h2.shape
    lane_spec = pl.BlockSpec((PEER_HEADS * PACKED_KEYS, EXP_TM), lambda t, e: (0, t), pipeline_mode=pl.Buffered(1))
    row_spec = pl.BlockSpec((1, PEER_HEADS, I1_PER_TILE, EXP_TM), lambda t, e: (e, 0, 0, t))
    return pl.pallas_call(
        _expert_kernel,
        grid=(n_tok // EXP_TM, N_EXPERTS // EXP_TE),
        in_specs=[
            pl.BlockSpec((EXP_TM, d), lambda t, e: (t, 0), pipeline_mode=pl.Buffered(1)),
            pl.BlockSpec((1, EXP_TE, d), lambda t, e: (layer, e, 0)),
            pl.BlockSpec((1, EXP_TE, d), lambda t, e: (layer, e, 0)),
            lane_spec, lane_spec, row_spec, row_spec,
        ],
        out_specs=pl.BlockSpec((EXP_TM, d), lambda t, e: (t, 0)),
        out_shape=jax.ShapeDtypeStruct((n_tok, d), F32),
        scratch_shapes=[pltpu.VMEM((EXP_TE, EXP_TM), F32), pltpu.VMEM((EXP_TE, EXP_TM), BF16)],
        compiler_params=pltpu.CompilerParams(
            dimension_semantics=("arbitrary", "arbitrary"), vmem_limit_bytes=58 * 2**20),
        name="peer_experts",
    )(h2, u, v, r2, e2, c, w1)


def _final_kernel(x_ref, acc_ref, mod_ref, o_ref):
    o_ref[0] = x_ref[0] + mod_ref[0, 5:6, :] * acc_ref[0]


def _final_call(x1, acc, mod):
    batch, seq, d = x1.shape
    tok_spec = pl.BlockSpec((1, FINAL_BLOCK, d), lambda b, i: (b, i, 0))
    return pl.pallas_call(
        _final_kernel,
        grid=(batch, seq // FINAL_BLOCK),
        in_specs=[tok_spec, tok_spec, pl.BlockSpec((1, 6, d), lambda b, i: (b, 0, 0))],
        out_specs=tok_spec,
        out_shape=jax.ShapeDtypeStruct((batch, seq, d), F32),
        compiler_params=pltpu.CompilerParams(dimension_semantics=("arbitrary", "arbitrary")),
        name="final_residual",
    )(x1, acc, mod)


def kernel(x, c, rel_bias, ada_w, ada_b, norm1_gain, w_in, q_gain, k_gain, sinks, w_pool, pool_scale, w_out,
           norm2_gain, peer_w_query, peer_sub_keys, peer_u, peer_v):
    depth = ada_w.shape[0]
    batch, seq, d = x.shape
    mod = _ada_call(c, ada_w, ada_b).reshape(depth, batch, 6, d)
    bias = _bias_call(rel_bias)
    acc = None
    for l in range(depth):
        x = _mix_call(
            x, acc, mod[l - 1] if l > 0 else None, mod[l], norm1_gain[l].reshape(1, d), w_in[l].astype(BF16),
            q_gain[l].reshape(1, HEAD_DIM), k_gain[l].reshape(1, HEAD_DIM), sinks[l], bias,
            w_pool[l].astype(BF16), pool_scale[l].reshape(1, D_POOL), w_out[l].astype(BF16))
        h2, r2, e2, cnt, w1 = _route_call(
            x, mod[l], norm2_gain[l].reshape(1, d), peer_w_query[l].astype(BF16), peer_sub_keys[l].astype(BF16))
        acc = _expert_call(l, h2.reshape(batch * seq, d), peer_u, peer_v, r2, e2, cnt, w1)
        acc = acc.reshape(batch, seq, d)
    return _final_call(x, acc, mod[depth - 1])
```

```python
import functools
import math

import jax
import jax.numpy as jnp
import numpy as np
from jax import lax
from jax.experimental import pallas as pl
from jax.experimental.pallas import tpu as pltpu

F32 = jnp.float32
BF16 = jnp.bfloat16

D_MODEL = 2048
CHUNK = 64
D_POOL = 1024
POOL_WINDOWS = (2, 4, 8, 16)
POOL_GROUP = D_POOL // len(POOL_WINDOWS)
D_ATTN = 1024
HEAD_DIM = 64
N_Q_HEADS = 16
N_KV_HEADS = 2
Q_PER_KV = N_Q_HEADS // N_KV_HEADS
WINDOW = 128
N_BUCKETS = 32
MAX_DISTANCE = 128
D_KV = N_KV_HEADS * HEAD_DIM
D_IN = D_POOL + D_ATTN + 2 * D_KV
PEER_HEADS = 8
N_KEYS = 128
N_EXPERTS = N_KEYS * N_KEYS
PEER_TOPK = 16
D_QUERY = 256
D_HALF = D_QUERY // 2
EPS = 1e-6

LANES = 128
SUBLANES = 8

ADA_TN = 1024
Q_BLOCK = 256
K_BLOCK = Q_BLOCK + WINDOW
POOL_HALO = 16
ROUTE_BLOCK = 256
EXP_TM = 1024
EXP_TE = 512
I1_PER_TILE = EXP_TE // N_KEYS
BF16_ROWS = 2 * SUBLANES
PACKED_KEYS = N_KEYS // 2
EXP_KCHUNK = 256
FINAL_BLOCK = 512

NEG_INF = float("-inf")


def _rms(xf, gain):
    return xf * lax.rsqrt(jnp.mean(xf * xf, axis=-1, keepdims=True) + EPS) * gain


def _ada_kernel(ct_ref, w_ref, b_ref, o_ref):
    w = w_ref[0]
    n_batch = ct_ref.shape[1]
    for b in range(n_batch):
        cb = ct_ref[:, b:b + 1]
        cb = cb * jax.nn.sigmoid(cb)
        o_ref[0, b:b + 1, :] = jnp.sum(w * cb, axis=0, keepdims=True) + b_ref[0]


def _ada_call(c, ada_w, ada_b):
    depth, d, n = ada_w.shape
    batch = c.shape[0]
    return pl.pallas_call(
        _ada_kernel,
        grid=(depth, n // ADA_TN),
        in_specs=[
            pl.BlockSpec((d, batch), lambda l, j: (0, 0)),
            pl.BlockSpec((1, d, ADA_TN), lambda l, j: (l, 0, j)),
            pl.BlockSpec((1, 1, ADA_TN), lambda l, j: (l, 0, j)),
        ],
        out_specs=pl.BlockSpec((1, batch, ADA_TN), lambda l, j: (l, 0, j)),
        out_shape=jax.ShapeDtypeStruct((depth, batch, n), F32),
        compiler_params=pltpu.CompilerParams(
            dimension_semantics=("arbitrary", "arbitrary"), vmem_limit_bytes=40 * 2**20),
        name="ada_mod",
    )(c.T, ada_w, ada_b.reshape(depth, 1, n))


def _t5_bucket(rel):
    nb = N_BUCKETS // 2
    max_exact = nb // 2
    ret = jnp.where(rel > 0, nb, 0)
    n = jnp.abs(rel)
    nf = jnp.maximum(n, 1).astype(jnp.float32)
    large = max_exact + (jnp.log(nf / max_exact) / math.log(MAX_DISTANCE / max_exact) * (nb - max_exact)).astype(jnp.int32)
    large = jnp.minimum(large, nb - 1)
    return ret + jnp.where(n < max_exact, n, large)


def _bias_kernel(rb_ref, bucket_ref, valid_ref, o_ref):
    h = pl.program_id(0)
    bucket = bucket_ref[...]
    acc = jnp.zeros(bucket.shape, F32)
    for b in range(N_BUCKETS):
        acc = jnp.where(bucket == b, rb_ref[b, h], acc)
    o_ref[0] = jnp.where(valid_ref[...] > 0, acc, NEG_INF)


def _bias_call(rel_bias):
    a = jnp.arange(Q_BLOCK)[:, None]
    bk = jnp.arange(K_BLOCK)[None, :]
    rel = bk - WINDOW - a
    bucket = _t5_bucket(rel).astype(jnp.int32)
    dchunk = (bk - WINDOW) // CHUNK - a // CHUNK
    valid = ((dchunk >= -(WINDOW // CHUNK)) & (dchunk <= 0)).astype(jnp.int32)
    return pl.pallas_call(
        _bias_kernel,
        grid=(N_Q_HEADS,),
        in_specs=[
            pl.BlockSpec(memory_space=pltpu.SMEM),
            pl.BlockSpec((Q_BLOCK, K_BLOCK), lambda h: (0, 0)),
            pl.BlockSpec((Q_BLOCK, K_BLOCK), lambda h: (0, 0)),
        ],
        out_specs=pl.BlockSpec((1, Q_BLOCK, K_BLOCK), lambda h: (h, 0, 0)),
        out_shape=jax.ShapeDtypeStruct((N_Q_HEADS, Q_BLOCK, K_BLOCK), F32),
        compiler_params=pltpu.CompilerParams(dimension_semantics=("arbitrary",)),
        name="rel_bias_tile",
    )(rel_bias.astype(F32), bucket, valid)


def _mix_kernel(has_acc, *refs):
    if has_acc:
        (x_ref, acc_ref, modp_ref, mod_ref, n1g_ref, w_in_ref, qg_ref, kg_ref, sinks_ref, bias_ref,
         w_pool_ref, ps_ref, w_out_ref, o_ref, p_ext, k_ext, v_ext, y_scr) = refs
    else:
        (x_ref, mod_ref, n1g_ref, w_in_ref, qg_ref, kg_ref, sinks_ref, bias_ref,
         w_pool_ref, ps_ref, w_out_ref, o_ref, p_ext, k_ext, v_ext, y_scr) = refs
    i = pl.program_id(1)
    x = x_ref[0]
    if has_acc:
        x = x + modp_ref[0, 5:6, :] * acc_ref[0]
    sh1 = mod_ref[0, 0:1, :]
    sc1 = mod_ref[0, 1:2, :]
    g1 = mod_ref[0, 2:3, :]
    h = _rms(x, n1g_ref[...]) * (1.0 + sc1) + sh1
    z = jnp.dot(h.astype(BF16), w_in_ref[...], preferred_element_type=F32)

    @pl.when(i == 0)
    def _():
        p_ext[0:POOL_HALO, :] = jnp.zeros((POOL_HALO, D_POOL), F32)
        k_ext[0:WINDOW, :] = jnp.zeros((WINDOW, D_KV), F32)
        v_ext[0:WINDOW, :] = jnp.zeros((WINDOW, D_KV), F32)

    @pl.when(i > 0)
    def _():
        p_ext[0:POOL_HALO, :] = p_ext[Q_BLOCK:Q_BLOCK + POOL_HALO, :]
        k_ext[0:WINDOW, :] = k_ext[Q_BLOCK:Q_BLOCK + WINDOW, :]
        v_ext[0:WINDOW, :] = v_ext[Q_BLOCK:Q_BLOCK + WINDOW, :]

    p_ext[POOL_HALO:POOL_HALO + Q_BLOCK, :] = z[:, 0:D_POOL]
    kg = kg_ref[...]
    for g in range(N_KV_HEADS):
        lo = D_POOL + D_ATTN + g * HEAD_DIM
        k_ext[WINDOW:WINDOW + Q_BLOCK, g * HEAD_DIM:(g + 1) * HEAD_DIM] = _rms(z[:, lo:lo + HEAD_DIM], kg)
    v_ext[WINDOW:WINDOW + Q_BLOCK, :] = z[:, D_POOL + D_ATTN + D_KV:D_IN]

    tpos = i * Q_BLOCK + lax.broadcasted_iota(jnp.int32, (Q_BLOCK, 1), 0)
    for g, w in enumerate(POOL_WINDOWS):
        c0 = g * POOL_GROUP
        cur = p_ext[POOL_HALO:POOL_HALO + Q_BLOCK, c0:c0 + POOL_GROUP]
        acc = cur
        for d in range(1, w):
            acc = acc + p_ext[POOL_HALO - d:POOL_HALO - d + Q_BLOCK, c0:c0 + POOL_GROUP]
        cnt = jnp.minimum(tpos + 1, w).astype(F32)
        pooled = acc / cnt - cur
        y = jnp.dot(pooled.astype(BF16), w_pool_ref[g], preferred_element_type=F32)
        y_scr[:, c0:c0 + POOL_GROUP] = y * ps_ref[:, c0:c0 + POOL_GROUP]

    qg = qg_ref[...]
    first_cols = lax.broadcasted_iota(jnp.int32, (Q_BLOCK, K_BLOCK), 1) < WINDOW
    hide = jnp.logical_and(first_cols, i == 0)
    for g in range(N_KV_HEADS):
        kn = k_ext[:, g * HEAD_DIM:(g + 1) * HEAD_DIM].astype(BF16)
        vv = v_ext[:, g * HEAD_DIM:(g + 1) * HEAD_DIM].astype(BF16)
        for hq in range(g * Q_PER_KV, (g + 1) * Q_PER_KV):
            lo = D_POOL + hq * HEAD_DIM
            qh = _rms(z[:, lo:lo + HEAD_DIM], qg).astype(BF16)
            s = lax.dot_general(qh, kn, (((1,), (1,)), ((), ())), preferred_element_type=F32)
            s = s * (1.0 / math.sqrt(HEAD_DIM)) + bias_ref[hq]
            s = jnp.where(hide, NEG_INF, s)
            sink = sinks_ref[hq]
            m = jnp.maximum(jnp.max(s, axis=-1, keepdims=True), sink)
            e = jnp.exp(s - m)
            den = jnp.sum(e, axis=-1, keepdims=True) + jnp.exp(sink - m)
            o = jnp.dot(e.astype(BF16), vv, preferred_element_type=F32) / den
            y_scr[:, D_POOL + hq * HEAD_DIM:D_POOL + (hq + 1) * HEAD_DIM] = o

    out = jnp.dot(y_scr[...].astype(BF16), w_out_ref[...], preferred_element_type=F32)
    o_ref[0] = x + g1 * out


def _const_spec(shape):
    nd = len(shape)
    return pl.BlockSpec(shape, lambda *_: (0,) * nd, pipeline_mode=pl.Buffered(1))


def _mix_call(x, acc, mod_prev, mod, n1g, w_in, qg, kg, sinks, bias, w_pool, ps, w_out):
    batch, seq, d = x.shape
    has_acc = acc is not None
    tok_spec = pl.BlockSpec((1, Q_BLOCK, d), lambda b, i: (b, i, 0))
    mod_spec = pl.BlockSpec((1, 6, d), lambda b, i: (b, 0, 0))
    in_specs = [tok_spec]
    args = [x]
    if has_acc:
        in_specs += [tok_spec, mod_spec]
        args += [acc, mod_prev]
    in_specs += [
        mod_spec,
        _const_spec((1, d)),
        _const_spec((d, D_IN)),
        _const_spec((1, HEAD_DIM)),
        _const_spec((1, HEAD_DIM)),
        pl.BlockSpec(memory_space=pltpu.SMEM),
        _const_spec((N_Q_HEADS, Q_BLOCK, K_BLOCK)),
        _const_spec((len(POOL_WINDOWS), POOL_GROUP, POOL_GROUP)),
        _const_spec((1, D_POOL)),
        _const_spec((d, d)),
    ]
    args += [mod, n1g, w_in, qg, kg, sinks, bias, w_pool, ps, w_out]
    return pl.pallas_call(
        functools.partial(_mix_kernel, has_acc),
        grid=(batch, seq // Q_BLOCK),
        in_specs=in_specs,
        out_specs=tok_spec,
        out_shape=jax.ShapeDtypeStruct((batch, seq, d), F32),
        scratch_shapes=[
            pltpu.VMEM((POOL_HALO + Q_BLOCK, D_POOL), F32),
            pltpu.VMEM((K_BLOCK, D_KV), F32),
            pltpu.VMEM((K_BLOCK, D_KV), F32),
            pltpu.VMEM((Q_BLOCK, d), F32),
        ],
        compiler_params=pltpu.CompilerParams(
            dimension_semantics=("arbitrary", "arbitrary"), vmem_limit_bytes=56 * 2**20),
        name="mixer",
    )(*args)


def _top16_rows(s, exact):
    rows = lax.broadcasted_iota(jnp.int32, s.shape, 0).astype(F32)
    rank = jnp.full(s.shape, float(PEER_TOPK), F32)
    cur = s
    vals = []
    for r in range(PEER_TOPK):
        m = jnp.max(cur, axis=0, keepdims=True)
        sel = cur == m
        if exact:
            first = jnp.min(jnp.where(sel, rows, float(N_KEYS)), axis=0, keepdims=True)
            sel = rows == first
        rank = jnp.where(sel, float(r), rank)
        cur = jnp.where(sel, NEG_INF, cur)
        vals.append(m)
    return rank, vals


def _stack16(vals):
    t = vals[0].shape[1]
    rows = lax.broadcasted_iota(jnp.int32, (PEER_TOPK, t), 0)
    out = jnp.zeros((PEER_TOPK, t), F32)
    for j, v in enumerate(vals):
        out = jnp.where(rows == j, v, out)
    return out


def _select_counts(xs, ys, exact):
    t = xs[0].shape[1]
    rows = lax.broadcasted_iota(jnp.int32, (PEER_TOPK, t), 0)
    rows_f = rows.astype(F32)
    x16 = _stack16(xs)
    y16 = _stack16(ys)
    cur, flat = [], []
    n_a = 4
    big = float(PEER_TOPK * PEER_TOPK)
    for k in range(n_a):
        ok = rows < PEER_TOPK // (k + 1)
        cur.append(jnp.where(ok, x16 + ys[k], NEG_INF))
        flat.append(jnp.where(ok, rows_f * float(PEER_TOPK) + float(k), big + 1.0))
    for j in range(3):
        ok = jnp.logical_and(rows >= n_a, rows < PEER_TOPK // (j + 1))
        cur.append(jnp.where(ok, y16 + xs[j], NEG_INF))
        flat.append(jnp.where(ok, rows_f + float(PEER_TOPK * j), big + 1.0))
    taken = [jnp.zeros((PEER_TOPK, t), F32) for _ in cur]
    top = xs[0] + ys[0]
    z = jnp.zeros((1, t), F32)
    for _ in range(PEER_TOPK):
        m = cur[0]
        for c in cur[1:]:
            m = jnp.maximum(m, c)
        m = jnp.max(m, axis=0, keepdims=True)
        if exact:
            f = jnp.where(cur[0] == m, flat[0], big)
            for c, fl in zip(cur[1:], flat[1:]):
                f = jnp.minimum(f, jnp.where(c == m, fl, big))
            f = jnp.min(f, axis=0, keepdims=True)
        for n in range(len(cur)):
            sel = (flat[n] == f) if exact else (cur[n] == m)
            cur[n] = jnp.where(sel, NEG_INF, cur[n])
            taken[n] = jnp.where(sel, 1.0, taken[n])
        z = z + jnp.exp(m - top)
    cnt = taken[0]
    for k in range(1, n_a):
        cnt = cnt + taken[k]
    for j in range(3):
        cnt = cnt + jnp.where(rows == j, jnp.sum(taken[n_a + j], axis=0, keepdims=True), 0.0)
    return cnt, z


def _route_kernel(x_ref, mod_ref, n2g_ref, wq_ref, keys_ref, h2_ref, r2_ref, e2_ref, c_ref, w1_ref, q_scr):
    x = x_ref[0]
    sh2 = mod_ref[0, 3:4, :]
    sc2 = mod_ref[0, 4:5, :]
    h2 = (_rms(x, n2g_ref[...]) * (1.0 + sc2) + sh2).astype(BF16)
    h2_ref[0] = h2
    q_scr[...] = jnp.dot(h2, wq_ref[...], preferred_element_type=F32)
    k1 = keys_ref[0]
    k2 = keys_ref[1]
    t = q_scr.shape[0]

    def head(hd, carry):
        off = pl.multiple_of(hd * D_QUERY, D_QUERY)
        q1 = q_scr[:, pl.ds(off, D_HALF)].astype(BF16)
        q2 = q_scr[:, pl.ds(off + D_HALF, D_HALF)].astype(BF16)
        dn = (((1,), (1,)), ((), ()))
        s1 = lax.dot_general(k1, q1, dn, preferred_element_type=F32)
        s2 = lax.dot_general(k2, q2, dn, preferred_element_type=F32)
        def route(exact):
            r1, xs = _top16_rows(s1, exact)
            r2, ys = _top16_rows(s2, exact)
            cnt, z = _select_counts(xs, ys, exact)
            c_full = jnp.zeros(s1.shape, F32)
            for j in range(PEER_TOPK):
                c_full = jnp.where(r1 == float(j), cnt[j:j + 1, :], c_full)
            w1 = jnp.exp(s1 - xs[0]) * (0.5 / z)
            head_rows = pl.ds(pl.multiple_of(hd * PACKED_KEYS, PACKED_KEYS), PACKED_KEYS)
            r2_ref[head_rows, :] = pltpu.bitcast(r2.astype(BF16), jnp.uint32)
            e2_ref[head_rows, :] = pltpu.bitcast(jnp.exp(s2 - ys[0]).astype(BF16), jnp.uint32)
            for blk in range(N_KEYS // I1_PER_TILE):
                rows = slice(blk * I1_PER_TILE, (blk + 1) * I1_PER_TILE)
                c_ref[blk, hd] = c_full[rows, :]
                w1_ref[blk, hd] = w1[rows, :]
            return r1, r2, cnt

        r1, r2, cnt = route(False)
        k16 = float(PEER_TOPK)
        n1 = jnp.sum(jnp.where(r1 < k16, 1.0, 0.0), axis=0, keepdims=True)
        n2 = jnp.sum(jnp.where(r2 < k16, 1.0, 0.0), axis=0, keepdims=True)
        nc = jnp.sum(cnt, axis=0, keepdims=True)
        ok = jnp.logical_and(jnp.logical_and(n1 == k16, n2 == k16), nc == k16)
        n_bad = jnp.sum(jnp.where(ok, 0.0, 1.0))

        @pl.when(n_bad > 0.0)
        def _():
            route(True)

        return carry

    lax.fori_loop(0, PEER_HEADS, head, 0)


def _route_call(x1, mod, n2g, wq, keys):
    batch, seq, d = x1.shape
    n_tok = batch * seq
    nblk = seq // ROUTE_BLOCK
    tok_spec = pl.BlockSpec((1, ROUTE_BLOCK, d), lambda b, i: (b, i, 0))
    lane_spec = pl.BlockSpec((PEER_HEADS * PACKED_KEYS, ROUTE_BLOCK), lambda b, i: (0, b * nblk + i))
    row_spec = pl.BlockSpec((N_KEYS // I1_PER_TILE, PEER_HEADS, I1_PER_TILE, ROUTE_BLOCK),
                            lambda b, i: (0, 0, 0, b * nblk + i))
    lane_shape = jax.ShapeDtypeStruct((PEER_HEADS * PACKED_KEYS, n_tok), jnp.uint32)
    row_shape = jax.ShapeDtypeStruct((N_KEYS // I1_PER_TILE, PEER_HEADS, I1_PER_TILE, n_tok), F32)
    return pl.pallas_call(
        _route_kernel,
        grid=(batch, nblk),
        in_specs=[
            tok_spec,
            pl.BlockSpec((1, 6, d), lambda b, i: (b, 0, 0)),
            _const_spec((1, d)),
            _const_spec((d, PEER_HEADS * D_QUERY)),
            _const_spec((2, N_KEYS, D_HALF)),
        ],
        out_specs=[tok_spec, lane_spec, lane_spec, row_spec, row_spec],
        out_shape=[jax.ShapeDtypeStruct((batch, seq, d), BF16), lane_shape, lane_shape, row_shape, row_shape],
        scratch_shapes=[pltpu.VMEM((ROUTE_BLOCK, PEER_HEADS * D_QUERY), F32)],
        compiler_params=pltpu.CompilerParams(
            dimension_semantics=("arbitrary", "arbitrary"), vmem_limit_bytes=48 * 2**20),
        name="peer_route",
    )(x1, mod, n2g, wq, keys)


def _expert_kernel(h2_ref, u_ref, v_ref, r2_ref, e2_ref, c_ref, w1_ref, o_ref, act_scr, g_scr):
    e = pl.program_id(1)

    @pl.when(e == 0)
    def _():
        o_ref[...] = jnp.zeros(o_ref.shape, F32)

    act_scr[...] = lax.dot_general(u_ref[0].astype(BF16), h2_ref[...], (((1,), (1,)), ((), ())),
                                   preferred_element_type=F32)
    for k in range(EXP_TE // EXP_KCHUNK):
        rows = slice(k * EXP_KCHUNK, (k + 1) * EXP_KCHUNK)
        _gate_chunk(r2_ref, e2_ref, c_ref, w1_ref, act_scr, g_scr,
                    k * EXP_KCHUNK // N_KEYS, (k + 1) * EXP_KCHUNK // N_KEYS)
        o_ref[...] += lax.dot_general(g_scr[rows, :], v_ref[0, rows, :].astype(BF16), (((0,), (0,)), ((), ())),
                                      preferred_element_type=F32)


def _gate_chunk(r2_ref, e2_ref, c_ref, w1_ref, act_r, g_w, a0, a1):
    for lc in range(EXP_TM // LANES):
        ls = slice(lc * LANES, (lc + 1) * LANES)
        for a in range(a0, a1):
            cb = [jnp.broadcast_to(c_ref[0, hd, a:a + 1, ls], (BF16_ROWS, LANES)).astype(BF16)
                  for hd in range(PEER_HEADS)]
            wb = [jnp.broadcast_to(w1_ref[0, hd, a:a + 1, ls], (BF16_ROWS, LANES)).astype(BF16)
                  for hd in range(PEER_HEADS)]
            for c2 in range(N_KEYS // BF16_ROWS):
                gate = None
                for hd in range(PEER_HEADS):
                    rs = slice(hd * PACKED_KEYS + c2 * SUBLANES, hd * PACKED_KEYS + (c2 + 1) * SUBLANES)
                    r2 = pltpu.bitcast(r2_ref[rs, ls], BF16)
                    e2 = pltpu.bitcast(e2_ref[rs, ls], BF16)
                    term = jnp.where(r2 < cb[hd], e2, jnp.zeros((), BF16)) * wb[hd]
                    gate = term if gate is None else gate + term
                es = slice(a * N_KEYS + c2 * BF16_ROWS, a * N_KEYS + (c2 + 1) * BF16_ROWS)
                act = act_r[es, ls]
                gl = act * (1.0 + lax.erf(act * (1.0 / math.sqrt(2.0))))
                g_w[es, ls] = gl.astype(BF16) * gate


def _expert_call(layer, h2, u, v, r2, e2, c, w1):
    n_tok, d = h2.shape
    lane_spec = pl.BlockSpec((PEER_HEADS * PACKED_KEYS, EXP_TM), lambda t, e: (0, t), pipeline_mode=pl.Buffered(1))
    row_spec = pl.BlockSpec((1, PEER_HEADS, I1_PER_TILE, EXP_TM), lambda t, e: (e, 0, 0, t))
    return pl.pallas_call(
        _expert_kernel,
        grid=(n_tok // EXP_TM, N_EXPERTS // EXP_TE),
        in_specs=[
            pl.BlockSpec((EXP_TM, d), lambda t, e: (t, 0), pipeline_mode=pl.Buffered(1)),
            pl.BlockSpec((1, EXP_TE, d), lambda t, e: (layer, e, 0)),
            pl.BlockSpec((1, EXP_TE, d), lambda t, e: (layer, e, 0)),
            lane_spec, lane_spec, row_spec, row_spec,
        ],
        out_specs=pl.BlockSpec((EXP_TM, d), lambda t, e: (t, 0)),
        out_shape=jax.ShapeDtypeStruct((n_tok, d), F32),
        scratch_shapes=[pltpu.VMEM((EXP_TE, EXP_TM), F32), pltpu.VMEM((EXP_TE, EXP_TM), BF16)],
        compiler_params=pltpu.CompilerParams(
            dimension_semantics=("arbitrary", "arbitrary"), vmem_limit_bytes=58 * 2**20),
        name="peer_experts",
    )(h2, u, v, r2, e2, c, w1)


def _final_kernel(x_ref, acc_ref, mod_ref, o_ref):
    o_ref[0] = x_ref[0] + mod_ref[0, 5:6, :] * acc_ref[0]


def _final_call(x1, acc, mod):
    batch, seq, d = x1.shape
    tok_spec = pl.BlockSpec((1, FINAL_BLOCK, d), lambda b, i: (b, i, 0))
    return pl.pallas_call(
        _final_kernel,
        grid=(batch, seq // FINAL_BLOCK),
        in_specs=[tok_spec, tok_spec, pl.BlockSpec((1, 6, d), lambda b, i: (b, 0, 0))],
        out_specs=tok_spec,
        out_shape=jax.ShapeDtypeStruct((batch, seq, d), F32),
        compiler_params=pltpu.CompilerParams(dimension_semantics=("arbitrary", "arbitrary")),
        name="final_residual",
    )(x1, acc, mod)


def kernel(x, c, rel_bias, ada_w, ada_b, norm1_gain, w_in, q_gain, k_gain, sinks, w_pool, pool_scale, w_out,
           norm2_gain, peer_w_query, peer_sub_keys, peer_u, peer_v):
    depth = ada_w.shape[0]
    batch, seq, d = x.shape
    mod = _ada_call(c, ada_w, ada_b).reshape(depth, batch, 6, d)
    bias = _bias_call(rel_bias)
    acc = None
    for l in range(depth):
        x = _mix_call(
            x, acc, mod[l - 1] if l > 0 else None, mod[l], norm1_gain[l].reshape(1, d), w_in[l].astype(BF16),
            q_gain[l].reshape(1, HEAD_DIM), k_gain[l].reshape(1, HEAD_DIM), sinks[l], bias,
            w_pool[l].astype(BF16), pool_scale[l].reshape(1, D_POOL), w_out[l].astype(BF16))
        h2, r2, e2, cnt, w1 = _route_call(
            x, mod[l], norm2_gain[l].reshape(1, d), peer_w_query[l].astype(BF16), peer_sub_keys[l].astype(BF16))
        acc = _expert_call(l, h2.reshape(batch * seq, d), peer_u, peer_v, r2, e2, cnt, w1)
        acc = acc.reshape(batch, seq, d)
    return _final_call(x, acc, mod[depth - 1])
```

```python
import functools
import math

import jax
import jax.numpy as jnp
import numpy as np
from jax import lax
from jax.experimental import pallas as pl
from jax.experimental.pallas import tpu as pltpu

F32 = jnp.float32
BF16 = jnp.bfloat16

D_MODEL = 2048
CHUNK = 64
D_POOL = 1024
POOL_WINDOWS = (2, 4, 8, 16)
POOL_GROUP = D_POOL // len(POOL_WINDOWS)
D_ATTN = 1024
HEAD_DIM = 64
N_Q_HEADS = 16
N_KV_HEADS = 2
Q_PER_KV = N_Q_HEADS // N_KV_HEADS
WINDOW = 128
N_BUCKETS = 32
MAX_DISTANCE = 128
D_KV = N_KV_HEADS * HEAD_DIM
D_IN = D_POOL + D_ATTN + 2 * D_KV
PEER_HEADS = 8
N_KEYS = 128
N_EXPERTS = N_KEYS * N_KEYS
PEER_TOPK = 16
D_QUERY = 256
D_HALF = D_QUERY // 2
EPS = 1e-6

LANES = 128
SUBLANES = 8

ADA_TN = 1024
Q_BLOCK = 256
K_BLOCK = Q_BLOCK + WINDOW
POOL_HALO = 16
ROUTE_BLOCK = 256
EXP_TM = 1024
EXP_TE = 512
I1_PER_TILE = EXP_TE // N_KEYS
BF16_ROWS = 2 * SUBLANES
PACKED_KEYS = N_KEYS // 2
EXP_KCHUNK = 512
FINAL_BLOCK = 512

NEG_INF = float("-inf")


def _rms(xf, gain):
    return xf * lax.rsqrt(jnp.mean(xf * xf, axis=-1, keepdims=True) + EPS) * gain


def _ada_kernel(ct_ref, w_ref, b_ref, o_ref):
    w = w_ref[0]
    n_batch = ct_ref.shape[1]
    for b in range(n_batch):
        cb = ct_ref[:, b:b + 1]
        cb = cb * jax.nn.sigmoid(cb)
        o_ref[0, b:b + 1, :] = jnp.sum(w * cb, axis=0, keepdims=True) + b_ref[0]


def _ada_call(c, ada_w, ada_b):
    depth, d, n = ada_w.shape
    batch = c.shape[0]
    return pl.pallas_call(
        _ada_kernel,
        grid=(depth, n // ADA_TN),
        in_specs=[
            pl.BlockSpec((d, batch), lambda l, j: (0, 0)),
            pl.BlockSpec((1, d, ADA_TN), lambda l, j: (l, 0, j)),
            pl.BlockSpec((1, 1, ADA_TN), lambda l, j: (l, 0, j)),
        ],
        out_specs=pl.BlockSpec((1, batch, ADA_TN), lambda l, j: (l, 0, j)),
        out_shape=jax.ShapeDtypeStruct((depth, batch, n), F32),
        compiler_params=pltpu.CompilerParams(
            dimension_semantics=("arbitrary", "arbitrary"), vmem_limit_bytes=40 * 2**20),
        name="ada_mod",
    )(c.T, ada_w, ada_b.reshape(depth, 1, n))


def _t5_bucket(rel):
    nb = N_BUCKETS // 2
    max_exact = nb // 2
    ret = jnp.where(rel > 0, nb, 0)
    n = jnp.abs(rel)
    nf = jnp.maximum(n, 1).astype(jnp.float32)
    large = max_exact + (jnp.log(nf / max_exact) / math.log(MAX_DISTANCE / max_exact) * (nb - max_exact)).astype(jnp.int32)
    large = jnp.minimum(large, nb - 1)
    return ret + jnp.where(n < max_exact, n, large)


def _bias_kernel(rb_ref, bucket_ref, valid_ref, o_ref):
    h = pl.program_id(0)
    bucket = bucket_ref[...]
    acc = jnp.zeros(bucket.shape, F32)
    for b in range(N_BUCKETS):
        acc = jnp.where(bucket == b, rb_ref[b, h], acc)
    o_ref[0] = jnp.where(valid_ref[...] > 0, acc, NEG_INF)


def _bias_call(rel_bias):
    a = jnp.arange(Q_BLOCK)[:, None]
    bk = jnp.arange(K_BLOCK)[None, :]
    rel = bk - WINDOW - a
    bucket = _t5_bucket(rel).astype(jnp.int32)
    dchunk = (bk - WINDOW) // CHUNK - a // CHUNK
    valid = ((dchunk >= -(WINDOW // CHUNK)) & (dchunk <= 0)).astype(jnp.int32)
    return pl.pallas_call(
        _bias_kernel,
        grid=(N_Q_HEADS,),
        in_specs=[
            pl.BlockSpec(memory_space=pltpu.SMEM),
            pl.BlockSpec((Q_BLOCK, K_BLOCK), lambda h: (0, 0)),
            pl.BlockSpec((Q_BLOCK, K_BLOCK), lambda h: (0, 0)),
        ],
        out_specs=pl.BlockSpec((1, Q_BLOCK, K_BLOCK), lambda h: (h, 0, 0)),
        out_shape=jax.ShapeDtypeStruct((N_Q_HEADS, Q_BLOCK, K_BLOCK), F32),
        compiler_params=pltpu.CompilerParams(dimension_semantics=("arbitrary",)),
        name="rel_bias_tile",
    )(rel_bias.astype(F32), bucket, valid)


def _mix_kernel(has_acc, *refs):
    if has_acc:
        (x_ref, acc_ref, modp_ref, mod_ref, n1g_ref, w_in_ref, qg_ref, kg_ref, sinks_ref, bias_ref,
         w_pool_ref, ps_ref, w_out_ref, o_ref, p_ext, k_ext, v_ext, y_scr) = refs
    else:
        (x_ref, mod_ref, n1g_ref, w_in_ref, qg_ref, kg_ref, sinks_ref, bias_ref,
         w_pool_ref, ps_ref, w_out_ref, o_ref, p_ext, k_ext, v_ext, y_scr) = refs
    i = pl.program_id(1)
    x = x_ref[0]
    if has_acc:
        x = x + modp_ref[0, 5:6, :] * acc_ref[0]
    sh1 = mod_ref[0, 0:1, :]
    sc1 = mod_ref[0, 1:2, :]
    g1 = mod_ref[0, 2:3, :]
    h = _rms(x, n1g_ref[...]) * (1.0 + sc1) + sh1
    z = jnp.dot(h.astype(BF16), w_in_ref[...], preferred_element_type=F32)

    @pl.when(i == 0)
    def _():
        p_ext[0:POOL_HALO, :] = jnp.zeros((POOL_HALO, D_POOL), F32)
        k_ext[0:WINDOW, :] = jnp.zeros((WINDOW, D_KV), F32)
        v_ext[0:WINDOW, :] = jnp.zeros((WINDOW, D_KV), F32)

    @pl.when(i > 0)
    def _():
        p_ext[0:POOL_HALO, :] = p_ext[Q_BLOCK:Q_BLOCK + POOL_HALO, :]
        k_ext[0:WINDOW, :] = k_ext[Q_BLOCK:Q_BLOCK + WINDOW, :]
        v_ext[0:WINDOW, :] = v_ext[Q_BLOCK:Q_BLOCK + WINDOW, :]

    p_ext[POOL_HALO:POOL_HALO + Q_BLOCK, :] = z[:, 0:D_POOL]
    kg = kg_ref[...]
    for g in range(N_KV_HEADS):
        lo = D_POOL + D_ATTN + g * HEAD_DIM
        k_ext[WINDOW:WINDOW + Q_BLOCK, g * HEAD_DIM:(g + 1) * HEAD_DIM] = _rms(z[:, lo:lo + HEAD_DIM], kg)
    v_ext[WINDOW:WINDOW + Q_BLOCK, :] = z[:, D_POOL + D_ATTN + D_KV:D_IN]

    tpos = i * Q_BLOCK + lax.broadcasted_iota(jnp.int32, (Q_BLOCK, 1), 0)
    for g, w in enumerate(POOL_WINDOWS):
        c0 = g * POOL_GROUP
        cur = p_ext[POOL_HALO:POOL_HALO + Q_BLOCK, c0:c0 + POOL_GROUP]
        acc = cur
        for d in range(1, w):
            acc = acc + p_ext[POOL_HALO - d:POOL_HALO - d + Q_BLOCK, c0:c0 + POOL_GROUP]
        cnt = jnp.minimum(tpos + 1, w).astype(F32)
        pooled = acc / cnt - cur
        y = jnp.dot(pooled.astype(BF16), w_pool_ref[g], preferred_element_type=F32)
        y_scr[:, c0:c0 + POOL_GROUP] = y * ps_ref[:, c0:c0 + POOL_GROUP]

    qg = qg_ref[...]
    first_cols = lax.broadcasted_iota(jnp.int32, (Q_BLOCK, K_BLOCK), 1) < WINDOW
    hide = jnp.logical_and(first_cols, i == 0)
    for g in range(N_KV_HEADS):
        kn = k_ext[:, g * HEAD_DIM:(g + 1) * HEAD_DIM].astype(BF16)
        vv = v_ext[:, g * HEAD_DIM:(g + 1) * HEAD_DIM].astype(BF16)
        for hq in range(g * Q_PER_KV, (g + 1) * Q_PER_KV):
            lo = D_POOL + hq * HEAD_DIM
            qh = _rms(z[:, lo:lo + HEAD_DIM], qg).astype(BF16)
            s = lax.dot_general(qh, kn, (((1,), (1,)), ((), ())), preferred_element_type=F32)
            s = s * (1.0 / math.sqrt(HEAD_DIM)) + bias_ref[hq]
            s = jnp.where(hide, NEG_INF, s)
            sink = sinks_ref[hq]
            m = jnp.maximum(jnp.max(s, axis=-1, keepdims=True), sink)
            e = jnp.exp(s - m)
            den = jnp.sum(e, axis=-1, keepdims=True) + jnp.exp(sink - m)
            o = jnp.dot(e.astype(BF16), vv, preferred_element_type=F32) / den
            y_scr[:, D_POOL + hq * HEAD_DIM:D_POOL + (hq + 1) * HEAD_DIM] = o

    out = jnp.dot(y_scr[...].astype(BF16), w_out_ref[...], preferred_element_type=F32)
    o_ref[0] = x + g1 * out


def _const_spec(shape):
    nd = len(shape)
    return pl.BlockSpec(shape, lambda *_: (0,) * nd, pipeline_mode=pl.Buffered(1))


def _mix_call(x, acc, mod_prev, mod, n1g, w_in, qg, kg, sinks, bias, w_pool, ps, w_out):
    batch, seq, d = x.shape
    has_acc = acc is not None
    tok_spec = pl.BlockSpec((1, Q_BLOCK, d), lambda b, i: (b, i, 0))
    mod_spec = pl.BlockSpec((1, 6, d), lambda b, i: (b, 0, 0))
    in_specs = [tok_spec]
    args = [x]
    if has_acc:
        in_specs += [tok_spec, mod_spec]
        args += [acc, mod_prev]
    in_specs += [
        mod_spec,
        _const_spec((1, d)),
        _const_spec((d, D_IN)),
        _const_spec((1, HEAD_DIM)),
        _const_spec((1, HEAD_DIM)),
        pl.BlockSpec(memory_space=pltpu.SMEM),
        _const_spec((N_Q_HEADS, Q_BLOCK, K_BLOCK)),
        _const_spec((len(POOL_WINDOWS), POOL_GROUP, POOL_GROUP)),
        _const_spec((1, D_POOL)),
        _const_spec((d, d)),
    ]
    args += [mod, n1g, w_in, qg, kg, sinks, bias, w_pool, ps, w_out]
    return pl.pallas_call(
        functools.partial(_mix_kernel, has_acc),
        grid=(batch, seq // Q_BLOCK),
        in_specs=in_specs,
        out_specs=tok_spec,
        out_shape=jax.ShapeDtypeStruct((batch, seq, d), F32),
        scratch_shapes=[
            pltpu.VMEM((POOL_HALO + Q_BLOCK, D_POOL), F32),
            pltpu.VMEM((K_BLOCK, D_KV), F32),
            pltpu.VMEM((K_BLOCK, D_KV), F32),
            pltpu.VMEM((Q_BLOCK, d), F32),
        ],
        compiler_params=pltpu.CompilerParams(
            dimension_semantics=("arbitrary", "arbitrary"), vmem_limit_bytes=56 * 2**20),
        name="mixer",
    )(*args)


def _top16_rows(s, exact):
    rows = lax.broadcasted_iota(jnp.int32, s.shape, 0).astype(F32)
    rank = jnp.full(s.shape, float(PEER_TOPK), F32)
    cur = s
    vals = []
    for r in range(PEER_TOPK):
        m = jnp.max(cur, axis=0, keepdims=True)
        sel = cur == m
        if exact:
            first = jnp.min(jnp.where(sel, rows, float(N_KEYS)), axis=0, keepdims=True)
            sel = rows == first
        rank = jnp.where(sel, float(r), rank)
        cur = jnp.where(sel, NEG_INF, cur)
        vals.append(m)
    return rank, vals


def _stack16(vals):
    t = vals[0].shape[1]
    rows = lax.broadcasted_iota(jnp.int32, (PEER_TOPK, t), 0)
    out = jnp.zeros((PEER_TOPK, t), F32)
    for j, v in enumerate(vals):
        out = jnp.where(rows == j, v, out)
    return out


def _select_counts(xs, ys, exact):
    t = xs[0].shape[1]
    rows = lax.broadcasted_iota(jnp.int32, (PEER_TOPK, t), 0)
    rows_f = rows.astype(F32)
    x16 = _stack16(xs)
    y16 = _stack16(ys)
    cur, flat = [], []
    n_a = 4
    big = float(PEER_TOPK * PEER_TOPK)
    for k in range(n_a):
        ok = rows < PEER_TOPK // (k + 1)
        cur.append(jnp.where(ok, x16 + ys[k], NEG_INF))
        flat.append(jnp.where(ok, rows_f * float(PEER_TOPK) + float(k), big + 1.0))
    for j in range(3):
        ok = jnp.logical_and(rows >= n_a, rows < PEER_TOPK // (j + 1))
        cur.append(jnp.where(ok, y16 + xs[j], NEG_INF))
        flat.append(jnp.where(ok, rows_f + float(PEER_TOPK * j), big + 1.0))
    taken = [jnp.zeros((PEER_TOPK, t), F32) for _ in cur]
    top = xs[0] + ys[0]
    z = jnp.zeros((1, t), F32)
    for _ in range(PEER_TOPK):
        m = cur[0]
        for c in cur[1:]:
            m = jnp.maximum(m, c)
        m = jnp.max(m, axis=0, keepdims=True)
        if exact:
            f = jnp.where(cur[0] == m, flat[0], big)
            for c, fl in zip(cur[1:], flat[1:]):
                f = jnp.minimum(f, jnp.where(c == m, fl, big))
            f = jnp.min(f, axis=0, keepdims=True)
        for n in range(len(cur)):
            sel = (flat[n] == f) if exact else (cur[n] == m)
            cur[n] = jnp.where(sel, NEG_INF, cur[n])
            taken[n] = jnp.where(sel, 1.0, taken[n])
        z = z + jnp.exp(m - top)
    cnt = taken[0]
    for k in range(1, n_a):
        cnt = cnt + taken[k]
    for j in range(3):
        cnt = cnt + jnp.where(rows == j, jnp.sum(taken[n_a + j], axis=0, keepdims=True), 0.0)
    return cnt, z


def _route_kernel(x_ref, mod_ref, n2g_ref, wq_ref, keys_ref, h2_ref, r2_ref, e2_ref, c_ref, w1_ref, q_scr):
    x = x_ref[0]
    sh2 = mod_ref[0, 3:4, :]
    sc2 = mod_ref[0, 4:5, :]
    h2 = (_rms(x, n2g_ref[...]) * (1.0 + sc2) + sh2).astype(BF16)
    h2_ref[0] = h2
    q_scr[...] = jnp.dot(h2, wq_ref[...], preferred_element_type=F32)
    k1 = keys_ref[0]
    k2 = keys_ref[1]
    t = q_scr.shape[0]

    def head(hd, carry):
        off = pl.multiple_of(hd * D_QUERY, D_QUERY)
        q1 = q_scr[:, pl.ds(off, D_HALF)].astype(BF16)
        q2 = q_scr[:, pl.ds(off + D_HALF, D_HALF)].astype(BF16)
        dn = (((1,), (1,)), ((), ()))
        s1 = lax.dot_general(k1, q1, dn, preferred_element_type=F32)
        s2 = lax.dot_general(k2, q2, dn, preferred_element_type=F32)
        def route(exact):
            r1, xs = _top16_rows(s1, exact)
            r2, ys = _top16_rows(s2, exact)
            cnt, z = _select_counts(xs, ys, exact)
            c_full = jnp.zeros(s1.shape, F32)
            for j in range(PEER_TOPK):
                c_full = jnp.where(r1 == float(j), cnt[j:j + 1, :], c_full)
            w1 = jnp.exp(s1 - xs[0]) * (0.5 / z)
            head_rows = pl.ds(pl.multiple_of(hd * PACKED_KEYS, PACKED_KEYS), PACKED_KEYS)
            r2_ref[head_rows, :] = pltpu.bitcast(r2.astype(BF16), jnp.uint32)
            e2_ref[head_rows, :] = pltpu.bitcast(jnp.exp(s2 - ys[0]).astype(BF16), jnp.uint32)
            for blk in range(N_KEYS // I1_PER_TILE):
                rows = slice(blk * I1_PER_TILE, (blk + 1) * I1_PER_TILE)
                c_ref[blk, hd] = c_full[rows, :]
                w1_ref[blk, hd] = w1[rows, :]
            return r1, r2, cnt

        r1, r2, cnt = route(False)
        k16 = float(PEER_TOPK)
        n1 = jnp.sum(jnp.where(r1 < k16, 1.0, 0.0), axis=0, keepdims=True)
        n2 = jnp.sum(jnp.where(r2 < k16, 1.0, 0.0), axis=0, keepdims=True)
        nc = jnp.sum(cnt, axis=0, keepdims=True)
        ok = jnp.logical_and(jnp.logical_and(n1 == k16, n2 == k16), nc == k16)
        n_bad = jnp.sum(jnp.where(ok, 0.0, 1.0))

        @pl.when(n_bad > 0.0)
        def _():
            route(True)

        return carry

    lax.fori_loop(0, PEER_HEADS, head, 0)


def _route_call(x1, mod, n2g, wq, keys):
    batch, seq, d = x1.shape
    n_tok = batch * seq
    nblk = seq // ROUTE_BLOCK
    tok_spec = pl.BlockSpec((1, ROUTE_BLOCK, d), lambda b, i: (b, i, 0))
    lane_spec = pl.BlockSpec((PEER_HEADS * PACKED_KEYS, ROUTE_BLOCK), lambda b, i: (0, b * nblk + i))
    row_spec = pl.BlockSpec((N_KEYS // I1_PER_TILE, PEER_HEADS, I1_PER_TILE, ROUTE_BLOCK),
                            lambda b, i: (0, 0, 0, b * nblk + i))
    lane_shape = jax.ShapeDtypeStruct((PEER_HEADS * PACKED_KEYS, n_tok), jnp.uint32)
    row_shape = jax.ShapeDtypeStruct((N_KEYS // I1_PER_TILE, PEER_HEADS, I1_PER_TILE, n_tok), F32)
    return pl.pallas_call(
        _route_kernel,
        grid=(batch, nblk),
        in_specs=[
            tok_spec,
            pl.BlockSpec((1, 6, d), lambda b, i: (b, 0, 0)),
            _const_spec((1, d)),
            _const_spec((d, PEER_HEADS * D_QUERY)),
            _const_spec((2, N_KEYS, D_HALF)),
        ],
        out_specs=[tok_spec, lane_spec, lane_spec, row_spec, row_spec],
        out_shape=[jax.ShapeDtypeStruct((batch, seq, d), BF16), lane_shape, lane_shape, row_shape, row_shape],
        scratch_shapes=[pltpu.VMEM((ROUTE_BLOCK, PEER_HEADS * D_QUERY), F32)],
        compiler_params=pltpu.CompilerParams(
            dimension_semantics=("arbitrary", "arbitrary"), vmem_limit_bytes=48 * 2**20),
        name="peer_route",
    )(x1, mod, n2g, wq, keys)


def _expert_kernel(h2_ref, u_ref, v_ref, r2_ref, e2_ref, c_ref, w1_ref, o_ref, act_scr, g_scr):
    e = pl.program_id(1)

    @pl.when(e == 0)
    def _():
        o_ref[...] = jnp.zeros(o_ref.shape, F32)

    act_scr[...] = lax.dot_general(u_ref[0].astype(BF16), h2_ref[...], (((1,), (1,)), ((), ())),
                                   preferred_element_type=F32)
    for k in range(EXP_TE // EXP_KCHUNK):
        rows = slice(k * EXP_KCHUNK, (k + 1) * EXP_KCHUNK)
        _gate_chunk(r2_ref, e2_ref, c_ref, w1_ref, act_scr, g_scr,
                    k * EXP_KCHUNK // N_KEYS, (k + 1) * EXP_KCHUNK // N_KEYS)
        o_ref[...] += lax.dot_general(g_scr[rows, :], v_ref[0, rows, :].astype(BF16), (((0,), (0,)), ((), ())),
                                      preferred_element_type=F32)


def _gate_chunk(r2_ref, e2_ref, c_ref, w1_ref, act_r, g_w, a0, a1):
    for lc in range(EXP_TM // LANES):
        ls = slice(lc * LANES, (lc + 1) * LANES)
        for a in range(a0, a1):
            cb = [jnp.broadcast_to(c_ref[0, hd, a:a + 1, ls], (BF16_ROWS, LANES)).astype(BF16)
                  for hd in range(PEER_HEADS)]
            wb = [jnp.broadcast_to(w1_ref[0, hd, a:a + 1, ls], (BF16_ROWS, LANES)).astype(BF16)
                  for hd in range(PEER_HEADS)]
            for c2 in range(N_KEYS // BF16_ROWS):
                gate = None
                for hd in range(PEER_HEADS):
                    rs = slice(hd * PACKED_KEYS + c2 * SUBLANES, hd * PACKED_KEYS + (c2 + 1) * SUBLANES)
                    r2 = pltpu.bitcast(r2_ref[rs, ls], BF16)
                    e2 = pltpu.bitcast(e2_ref[rs, ls], BF16)
                    term = jnp.where(r2 < cb[hd], e2, jnp.zeros((), BF16)) * wb[hd]
                    gate = term if gate is None else gate + term
                es = slice(a * N_KEYS + c2 * BF16_ROWS, a * N_KEYS + (c2 + 1) * BF16_ROWS)
                act = act_r[es, ls]
                gl = act * (1.0 + lax.erf(act * (1.0 / math.sqrt(2.0))))
                g_w[es, ls] = gl.astype(BF16) * gate


def _expert_call(layer, h2, u, v, r2, e2, c, w1):
    n_tok, d = h2.shape
    lane_spec = pl.BlockSpec((PEER_HEADS * PACKED_KEYS, EXP_TM), lambda t, e: (0, t), pipeline_mode=pl.Buffered(1))
    row_spec = pl.BlockSpec((1, PEER_HEADS, I1_PER_TILE, EXP_TM), lambda t, e: (e, 0, 0, t))
    return pl.pallas_call(
        _expert_kernel,
        grid=(n_tok // EXP_TM, N_EXPERTS // EXP_TE),
        in_specs=[
            pl.BlockSpec((EXP_TM, d), lambda t, e: (t, 0), pipeline_mode=pl.Buffered(1)),
            pl.BlockSpec((1, EXP_TE, d), lambda t, e: (layer, e, 0)),
            pl.BlockSpec((1, EXP_TE, d), lambda t, e: (layer, e, 0)),
            lane_spec, lane_spec, row_spec, row_spec,
        ],
        out_specs=pl.BlockSpec((EXP_TM, d), lambda t, e: (t, 0)),
        out_shape=jax.ShapeDtypeStruct((n_tok, d), F32),
        scratch_shapes=[pltpu.VMEM((EXP_TE, EXP_TM), F32), pltpu.VMEM((EXP_TE, EXP_TM), BF16)],
        compiler_params=pltpu.CompilerParams(
            dimension_semantics=("arbitrary", "arbitrary"), vmem_limit_bytes=58 * 2**20),
        name="peer_experts",
    )(h2, u, v, r2, e2, c, w1)


def _final_kernel(x_ref, acc_ref, mod_ref, o_ref):
    o_ref[0] = x_ref[0] + mod_ref[0, 5:6, :] * acc_ref[0]


def _final_call(x1, acc, mod):
    batch, seq, d = x1.shape
    tok_spec = pl.BlockSpec((1, FINAL_BLOCK, d), lambda b, i: (b, i, 0))
    return pl.pallas_call(
        _final_kernel,
        grid=(batch, seq // FINAL_BLOCK),
        in_specs=[tok_spec, tok_spec, pl.BlockSpec((1, 6, d), lambda b, i: (b, 0, 0))],
        out_specs=tok_spec,
        out_shape=jax.ShapeDtypeStruct((batch, seq, d), F32),
        compiler_params=pltpu.CompilerParams(dimension_semantics=("arbitrary", "arbitrary")),
        name="final_residual",
    )(x1, acc, mod)


def kernel(x, c, rel_bias, ada_w, ada_b, norm1_gain, w_in, q_gain, k_gain, sinks, w_pool, pool_scale, w_out,
           norm2_gain, peer_w_query, peer_sub_keys, peer_u, peer_v):
    depth = ada_w.shape[0]
    batch, seq, d = x.shape
    mod = _ada_call(c, ada_w, ada_b).reshape(depth, batch, 6, d)
    bias = _bias_call(rel_bias)
    acc = None
    for l in range(depth):
        x = _mix_call(
            x, acc, mod[l - 1] if l > 0 else None, mod[l], norm1_gain[l].reshape(1, d), w_in[l].astype(BF16),
            q_gain[l].reshape(1, HEAD_DIM), k_gain[l].reshape(1, HEAD_DIM), sinks[l], bias,
            w_pool[l].astype(BF16), pool_scale[l].reshape(1, D_POOL), w_out[l].astype(BF16))
        h2, r2, e2, cnt, w1 = _route_call(
            x, mod[l], norm2_gain[l].reshape(1, d), peer_w_query[l].astype(BF16), peer_sub_keys[l].astype(BF16))
        acc = _expert_call(l, h2.reshape(batch * seq, d), peer_u, peer_v, r2, e2, cnt, w1)
        acc = acc.reshape(batch, seq, d)
    return _final_call(x, acc, mod[depth - 1])
```

```python
import functools
import math

import jax
import jax.numpy as jnp
import numpy as np
from jax import lax
from jax.experimental import pallas as pl
from jax.experimental.pallas import tpu as pltpu

F32 = jnp.float32
BF16 = jnp.bfloat16

D_MODEL = 2048
CHUNK = 64
D_POOL = 1024
POOL_WINDOWS = (2, 4, 8, 16)
POOL_GROUP = D_POOL // len(POOL_WINDOWS)
D_ATTN = 1024
HEAD_DIM = 64
N_Q_HEADS = 16
N_KV_HEADS = 2
Q_PER_KV = N_Q_HEADS // N_KV_HEADS
WINDOW = 128
N_BUCKETS = 32
MAX_DISTANCE = 128
D_KV = N_KV_HEADS * HEAD_DIM
D_IN = D_POOL + D_ATTN + 2 * D_KV
PEER_HEADS = 8
N_KEYS = 128
N_EXPERTS = N_KEYS * N_KEYS
PEER_TOPK = 16
D_QUERY = 256
D_HALF = D_QUERY // 2
EPS = 1e-6

LANES = 128
SUBLANES = 8

ADA_TN = 1024
Q_BLOCK = 256
K_BLOCK = Q_BLOCK + WINDOW
POOL_HALO = 16
ROUTE_BLOCK = 256
EXP_TM = 1024
EXP_TE = 512
I1_PER_TILE = EXP_TE // N_KEYS
BF16_ROWS = 2 * SUBLANES
PACKED_KEYS = N_KEYS // 2
EXP_KCHUNK = 512
FINAL_BLOCK = 512

NEG_INF = float("-inf")


def _rms(xf, gain):
    return xf * lax.rsqrt(jnp.mean(xf * xf, axis=-1, keepdims=True) + EPS) * gain


def _ada_kernel(ct_ref, w_ref, b_ref, o_ref):
    w = w_ref[0]
    n_batch = ct_ref.shape[1]
    for b in range(n_batch):
        cb = ct_ref[:, b:b + 1]
        cb = cb * jax.nn.sigmoid(cb)
        o_ref[0, b:b + 1, :] = jnp.sum(w * cb, axis=0, keepdims=True) + b_ref[0]


def _ada_call(c, ada_w, ada_b):
    depth, d, n = ada_w.shape
    batch = c.shape[0]
    return pl.pallas_call(
        _ada_kernel,
        grid=(depth, n // ADA_TN),
        in_specs=[
            pl.BlockSpec((d, batch), lambda l, j: (0, 0)),
            pl.BlockSpec((1, d, ADA_TN), lambda l, j: (l, 0, j)),
            pl.BlockSpec((1, 1, ADA_TN), lambda l, j: (l, 0, j)),
        ],
        out_specs=pl.BlockSpec((1, batch, ADA_TN), lambda l, j: (l, 0, j)),
        out_shape=jax.ShapeDtypeStruct((depth, batch, n), F32),
        compiler_params=pltpu.CompilerParams(
            dimension_semantics=("arbitrary", "arbitrary"), vmem_limit_bytes=40 * 2**20),
        name="ada_mod",
    )(c.T, ada_w, ada_b.reshape(depth, 1, n))


def _t5_bucket(rel):
    nb = N_BUCKETS // 2
    max_exact = nb // 2
    ret = jnp.where(rel > 0, nb, 0)
    n = jnp.abs(rel)
    nf = jnp.maximum(n, 1).astype(jnp.float32)
    large = max_exact + (jnp.log(nf / max_exact) / math.log(MAX_DISTANCE / max_exact) * (nb - max_exact)).astype(jnp.int32)
    large = jnp.minimum(large, nb - 1)
    return ret + jnp.where(n < max_exact, n, large)


def _bias_kernel(rb_ref, bucket_ref, valid_ref, o_ref):
    h = pl.program_id(0)
    bucket = bucket_ref[...]
    acc = jnp.zeros(bucket.shape, F32)
    for b in range(N_BUCKETS):
        acc = jnp.where(bucket == b, rb_ref[b, h], acc)
    o_ref[0] = jnp.where(valid_ref[...] > 0, acc, NEG_INF)


def _bias_call(rel_bias):
    a = jnp.arange(Q_BLOCK)[:, None]
    bk = jnp.arange(K_BLOCK)[None, :]
    rel = bk - WINDOW - a
    bucket = _t5_bucket(rel).astype(jnp.int32)
    dchunk = (bk - WINDOW) // CHUNK - a // CHUNK
    valid = ((dchunk >= -(WINDOW // CHUNK)) & (dchunk <= 0)).astype(jnp.int32)
    return pl.pallas_call(
        _bias_kernel,
        grid=(N_Q_HEADS,),
        in_specs=[
            pl.BlockSpec(memory_space=pltpu.SMEM),
            pl.BlockSpec((Q_BLOCK, K_BLOCK), lambda h: (0, 0)),
            pl.BlockSpec((Q_BLOCK, K_BLOCK), lambda h: (0, 0)),
        ],
        out_specs=pl.BlockSpec((1, Q_BLOCK, K_BLOCK), lambda h: (h, 0, 0)),
        out_shape=jax.ShapeDtypeStruct((N_Q_HEADS, Q_BLOCK, K_BLOCK), F32),
        compiler_params=pltpu.CompilerParams(dimension_semantics=("arbitrary",)),
        name="rel_bias_tile",
    )(rel_bias.astype(F32), bucket, valid)


def _mix_kernel(has_acc, *refs):
    if has_acc:
        (x_ref, acc_ref, modp_ref, mod_ref, n1g_ref, w_in_ref, qg_ref, kg_ref, sinks_ref, bias_ref,
         w_pool_ref, ps_ref, w_out_ref, o_ref, p_ext, k_ext, v_ext, y_scr, s_scr, e_scr, qn_scr) = refs
    else:
        (x_ref, mod_ref, n1g_ref, w_in_ref, qg_ref, kg_ref, sinks_ref, bias_ref,
         w_pool_ref, ps_ref, w_out_ref, o_ref, p_ext, k_ext, v_ext, y_scr, s_scr, e_scr, qn_scr) = refs
    i = pl.program_id(1)
    x = x_ref[0]
    if has_acc:
        x = x + modp_ref[0, 5:6, :] * acc_ref[0]
    sh1 = mod_ref[0, 0:1, :]
    sc1 = mod_ref[0, 1:2, :]
    g1 = mod_ref[0, 2:3, :]
    h = _rms(x, n1g_ref[...]) * (1.0 + sc1) + sh1
    z = jnp.dot(h.astype(BF16), w_in_ref[...], preferred_element_type=F32)

    @pl.when(i == 0)
    def _():
        p_ext[0:POOL_HALO, :] = jnp.zeros((POOL_HALO, D_POOL), F32)
        k_ext[0:WINDOW, :] = jnp.zeros((WINDOW, D_KV), F32)
        v_ext[0:WINDOW, :] = jnp.zeros((WINDOW, D_KV), F32)

    @pl.when(i > 0)
    def _():
        p_ext[0:POOL_HALO, :] = p_ext[Q_BLOCK:Q_BLOCK + POOL_HALO, :]
        k_ext[0:WINDOW, :] = k_ext[Q_BLOCK:Q_BLOCK + WINDOW, :]
        v_ext[0:WINDOW, :] = v_ext[Q_BLOCK:Q_BLOCK + WINDOW, :]

    p_ext[POOL_HALO:POOL_HALO + Q_BLOCK, :] = z[:, 0:D_POOL]
    kg = kg_ref[...]
    for g in range(N_KV_HEADS):
        lo = D_POOL + D_ATTN + g * HEAD_DIM
        k_ext[WINDOW:WINDOW + Q_BLOCK, g * HEAD_DIM:(g + 1) * HEAD_DIM] = _rms(z[:, lo:lo + HEAD_DIM], kg)
    v_ext[WINDOW:WINDOW + Q_BLOCK, :] = z[:, D_POOL + D_ATTN + D_KV:D_IN]

    tpos = i * Q_BLOCK + lax.broadcasted_iota(jnp.int32, (Q_BLOCK, 1), 0)
    for g, w in enumerate(POOL_WINDOWS):
        c0 = g * POOL_GROUP
        cur = p_ext[POOL_HALO:POOL_HALO + Q_BLOCK, c0:c0 + POOL_GROUP]
        acc = cur
        for d in range(1, w):
            acc = acc + p_ext[POOL_HALO - d:POOL_HALO - d + Q_BLOCK, c0:c0 + POOL_GROUP]
        cnt = jnp.minimum(tpos + 1, w).astype(F32)
        pooled = acc / cnt - cur
        y = jnp.dot(pooled.astype(BF16), w_pool_ref[g], preferred_element_type=F32)
        y_scr[:, c0:c0 + POOL_GROUP] = y * ps_ref[:, c0:c0 + POOL_GROUP]

    qg = qg_ref[...]
    first_cols = lax.broadcasted_iota(jnp.int32, (Q_BLOCK, K_BLOCK), 1) < WINDOW
    hide = jnp.logical_and(first_cols, i == 0)
    kn = [k_ext[:, g * HEAD_DIM:(g + 1) * HEAD_DIM].astype(BF16) for g in range(N_KV_HEADS)]
    vv = [v_ext[:, g * HEAD_DIM:(g + 1) * HEAD_DIM].astype(BF16) for g in range(N_KV_HEADS)]
    for hq in range(N_Q_HEADS):
        lo = D_POOL + hq * HEAD_DIM
        qn_scr[:, hq * HEAD_DIM:(hq + 1) * HEAD_DIM] = _rms(z[:, lo:lo + HEAD_DIM], qg).astype(BF16)
    for hq in range(N_Q_HEADS):
        qh = qn_scr[:, hq * HEAD_DIM:(hq + 1) * HEAD_DIM]
        s = lax.dot_general(qh, kn[hq // Q_PER_KV], (((1,), (1,)), ((), ())), preferred_element_type=F32)
        s = s * (1.0 / math.sqrt(HEAD_DIM)) + bias_ref[hq]
        s_scr[hq] = jnp.where(hide, NEG_INF, s)
    for hq in range(N_Q_HEADS):
        s = s_scr[hq]
        sink = sinks_ref[hq]
        m = jnp.maximum(jnp.max(s, axis=-1, keepdims=True), sink)
        e = jnp.exp(s - m)
        den = jnp.sum(e, axis=-1, keepdims=True) + jnp.exp(sink - m)
        e_scr[hq] = (e * (1.0 / den)).astype(BF16)
    for hq in range(N_Q_HEADS):
        o = jnp.dot(e_scr[hq], vv[hq // Q_PER_KV], preferred_element_type=F32)
        y_scr[:, D_POOL + hq * HEAD_DIM:D_POOL + (hq + 1) * HEAD_DIM] = o

    out = jnp.dot(y_scr[...].astype(BF16), w_out_ref[...], preferred_element_type=F32)
    o_ref[0] = x + g1 * out


def _const_spec(shape):
    nd = len(shape)
    return pl.BlockSpec(shape, lambda *_: (0,) * nd, pipeline_mode=pl.Buffered(1))


def _mix_call(x, acc, mod_prev, mod, n1g, w_in, qg, kg, sinks, bias, w_pool, ps, w_out):
    batch, seq, d = x.shape
    has_acc = acc is not None
    tok_spec = pl.BlockSpec((1, Q_BLOCK, d), lambda b, i: (b, i, 0))
    mod_spec = pl.BlockSpec((1, 6, d), lambda b, i: (b, 0, 0))
    in_specs = [tok_spec]
    args = [x]
    if has_acc:
        in_specs += [tok_spec, mod_spec]
        args += [acc, mod_prev]
    in_specs += [
        mod_spec,
        _const_spec((1, d)),
        _const_spec((d, D_IN)),
        _const_spec((1, HEAD_DIM)),
        _const_spec((1, HEAD_DIM)),
        pl.BlockSpec(memory_space=pltpu.SMEM),
        _const_spec((N_Q_HEADS, Q_BLOCK, K_BLOCK)),
        _const_spec((len(POOL_WINDOWS), POOL_GROUP, POOL_GROUP)),
        _const_spec((1, D_POOL)),
        _const_spec((d, d)),
    ]
    args += [mod, n1g, w_in, qg, kg, sinks, bias, w_pool, ps, w_out]
    return pl.pallas_call(
        functools.partial(_mix_kernel, has_acc),
        grid=(batch, seq // Q_BLOCK),
        in_specs=in_specs,
        out_specs=tok_spec,
        out_shape=jax.ShapeDtypeStruct((batch, seq, d), F32),
        scratch_shapes=[
            pltpu.VMEM((POOL_HALO + Q_BLOCK, D_POOL), F32),
            pltpu.VMEM((K_BLOCK, D_KV), F32),
            pltpu.VMEM((K_BLOCK, D_KV), F32),
            pltpu.VMEM((Q_BLOCK, d), F32),
            pltpu.VMEM((N_Q_HEADS, Q_BLOCK, K_BLOCK), F32),
            pltpu.VMEM((N_Q_HEADS, Q_BLOCK, K_BLOCK), BF16),
            pltpu.VMEM((Q_BLOCK, D_ATTN), BF16),
        ],
        compiler_params=pltpu.CompilerParams(
            dimension_semantics=("arbitrary", "arbitrary"), vmem_limit_bytes=56 * 2**20),
        name="mixer",
    )(*args)


def _top16_rows(s, exact):
    rows = lax.broadcasted_iota(jnp.int32, s.shape, 0).astype(F32)
    rank = jnp.full(s.shape, float(PEER_TOPK), F32)
    cur = s
    vals = []
    for r in range(PEER_TOPK):
        m = jnp.max(cur, axis=0, keepdims=True)
        sel = cur == m
        if exact:
            first = jnp.min(jnp.where(sel, rows, float(N_KEYS)), axis=0, keepdims=True)
            sel = rows == first
        rank = jnp.where(sel, float(r), rank)
        cur = jnp.where(sel, NEG_INF, cur)
        vals.append(m)
    return rank, vals


def _stack16(vals):
    t = vals[0].shape[1]
    rows = lax.broadcasted_iota(jnp.int32, (PEER_TOPK, t), 0)
    out = jnp.zeros((PEER_TOPK, t), F32)
    for j, v in enumerate(vals):
        out = jnp.where(rows == j, v, out)
    return out


def _select_counts(xs, ys, exact):
    t = xs[0].shape[1]
    rows = lax.broadcasted_iota(jnp.int32, (PEER_TOPK, t), 0)
    rows_f = rows.astype(F32)
    x16 = _stack16(xs)
    y16 = _stack16(ys)
    cur, flat = [], []
    n_a = 4
    big = float(PEER_TOPK * PEER_TOPK)
    for k in range(n_a):
        ok = rows < PEER_TOPK // (k + 1)
        cur.append(jnp.where(ok, x16 + ys[k], NEG_INF))
        flat.append(jnp.where(ok, rows_f * float(PEER_TOPK) + float(k), big + 1.0))
    for j in range(3):
        ok = jnp.logical_and(rows >= n_a, rows < PEER_TOPK // (j + 1))
        cur.append(jnp.where(ok, y16 + xs[j], NEG_INF))
        flat.append(jnp.where(ok, rows_f + float(PEER_TOPK * j), big + 1.0))
    taken = [jnp.zeros((PEER_TOPK, t), F32) for _ in cur]
    top = xs[0] + ys[0]
    z = jnp.zeros((1, t), F32)
    for _ in range(PEER_TOPK):
        m = cur[0]
        for c in cur[1:]:
            m = jnp.maximum(m, c)
        m = jnp.max(m, axis=0, keepdims=True)
        if exact:
            f = jnp.where(cur[0] == m, flat[0], big)
            for c, fl in zip(cur[1:], flat[1:]):
                f = jnp.minimum(f, jnp.where(c == m, fl, big))
            f = jnp.min(f, axis=0, keepdims=True)
        for n in range(len(cur)):
            sel = (flat[n] == f) if exact else (cur[n] == m)
            cur[n] = jnp.where(sel, NEG_INF, cur[n])
            taken[n] = jnp.where(sel, 1.0, taken[n])
        z = z + jnp.exp(m - top)
    cnt = taken[0]
    for k in range(1, n_a):
        cnt = cnt + taken[k]
    for j in range(3):
        cnt = cnt + jnp.where(rows == j, jnp.sum(taken[n_a + j], axis=0, keepdims=True), 0.0)
    return cnt, z


def _route_kernel(x_ref, mod_ref, n2g_ref, wq_ref, keys_ref, h2_ref, r2_ref, e2_ref, c_ref, w1_ref, q_scr):
    x = x_ref[0]
    sh2 = mod_ref[0, 3:4, :]
    sc2 = mod_ref[0, 4:5, :]
    h2 = (_rms(x, n2g_ref[...]) * (1.0 + sc2) + sh2).astype(BF16)
    h2_ref[0] = h2
    q_scr[...] = jnp.dot(h2, wq_ref[...], preferred_element_type=F32)
    k1 = keys_ref[0]
    k2 = keys_ref[1]
    t = q_scr.shape[0]

    def head(hd, carry):
        off = pl.multiple_of(hd * D_QUERY, D_QUERY)
        q1 = q_scr[:, pl.ds(off, D_HALF)].astype(BF16)
        q2 = q_scr[:, pl.ds(off + D_HALF, D_HALF)].astype(BF16)
        dn = (((1,), (1,)), ((), ()))
        s1 = lax.dot_general(k1, q1, dn, preferred_element_type=F32)
        s2 = lax.dot_general(k2, q2, dn, preferred_element_type=F32)
        def route(exact):
            r1, xs = _top16_rows(s1, exact)
            r2, ys = _top16_rows(s2, exact)
            cnt, z = _select_counts(xs, ys, exact)
            c_full = jnp.zeros(s1.shape, F32)
            for j in range(PEER_TOPK):
                c_full = jnp.where(r1 == float(j), cnt[j:j + 1, :], c_full)
            w1 = jnp.exp(s1 - xs[0]) * (0.5 / z)
            head_rows = pl.ds(pl.multiple_of(hd * PACKED_KEYS, PACKED_KEYS), PACKED_KEYS)
            r2_ref[head_rows, :] = pltpu.bitcast(r2.astype(BF16), jnp.uint32)
            e2_ref[head_rows, :] = pltpu.bitcast(jnp.exp(s2 - ys[0]).astype(BF16), jnp.uint32)
            for blk in range(N_KEYS // I1_PER_TILE):
                rows = slice(blk * I1_PER_TILE, (blk + 1) * I1_PER_TILE)
                c_ref[blk, hd] = c_full[rows, :]
                w1_ref[blk, hd] = w1[rows, :]
            return r1, r2, cnt

        r1, r2, cnt = route(False)
        k16 = float(PEER_TOPK)
        n1 = jnp.sum(jnp.where(r1 < k16, 1.0, 0.0), axis=0, keepdims=True)
        n2 = jnp.sum(jnp.where(r2 < k16, 1.0, 0.0), axis=0, keepdims=True)
        nc = jnp.sum(cnt, axis=0, keepdims=True)
        ok = jnp.logical_and(jnp.logical_and(n1 == k16, n2 == k16), nc == k16)
        n_bad = jnp.sum(jnp.where(ok, 0.0, 1.0))

        @pl.when(n_bad > 0.0)
        def _():
            route(True)

        return carry

    lax.fori_loop(0, PEER_HEADS, head, 0)


def _route_call(x1, mod, n2g, wq, keys):
    batch, seq, d = x1.shape
    n_tok = batch * seq
    nblk = seq // ROUTE_BLOCK
    tok_spec = pl.BlockSpec((1, ROUTE_BLOCK, d), lambda b, i: (b, i, 0))
    lane_spec = pl.BlockSpec((PEER_HEADS * PACKED_KEYS, ROUTE_BLOCK), lambda b, i: (0, b * nblk + i))
    row_spec = pl.BlockSpec((N_KEYS // I1_PER_TILE, PEER_HEADS, I1_PER_TILE, ROUTE_BLOCK),
                            lambda b, i: (0, 0, 0, b * nblk + i))
    lane_shape = jax.ShapeDtypeStruct((PEER_HEADS * PACKED_KEYS, n_tok), jnp.uint32)
    row_shape = jax.ShapeDtypeStruct((N_KEYS // I1_PER_TILE, PEER_HEADS, I1_PER_TILE, n_tok), F32)
    return pl.pallas_call(
        _route_kernel,
        grid=(batch, nblk),
        in_specs=[
            tok_spec,
            pl.BlockSpec((1, 6, d), lambda b, i: (b, 0, 0)),
            _const_spec((1, d)),
            _const_spec((d, PEER_HEADS * D_QUERY)),
            _const_spec((2, N_KEYS, D_HALF)),
        ],
        out_specs=[tok_spec, lane_spec, lane_spec, row_spec, row_spec],
        out_shape=[jax.ShapeDtypeStruct((batch, seq, d), BF16), lane_shape, lane_shape, row_shape, row_shape],
        scratch_shapes=[pltpu.VMEM((ROUTE_BLOCK, PEER_HEADS * D_QUERY), F32)],
        compiler_params=pltpu.CompilerParams(
            dimension_semantics=("arbitrary", "arbitrary"), vmem_limit_bytes=48 * 2**20),
        name="peer_route",
    )(x1, mod, n2g, wq, keys)


def _expert_kernel(h2_ref, u_ref, v_ref, r2_ref, e2_ref, c_ref, w1_ref, o_ref, act_scr, g_scr, acc_t):
    e = pl.program_id(1)

    @pl.when(e == 0)
    def _():
        acc_t[...] = jnp.zeros(acc_t.shape, F32)

    act_scr[...] = lax.dot_general(u_ref[0].astype(BF16), h2_ref[...], (((1,), (1,)), ((), ())),
                                   preferred_element_type=F32)
    _gate_chunk(r2_ref, e2_ref, c_ref, w1_ref, act_scr, g_scr, 0, I1_PER_TILE)
    acc_t[...] += lax.dot_general(v_ref[0].astype(BF16), g_scr[...], (((0,), (0,)), ((), ())),
                                  preferred_element_type=F32)

    @pl.when(e == pl.num_programs(1) - 1)
    def _():
        o_ref[...] = acc_t[...].T


def _gate_chunk(r2_ref, e2_ref, c_ref, w1_ref, act_r, g_w, a0, a1):
    for lc in range(EXP_TM // LANES):
        ls = slice(lc * LANES, (lc + 1) * LANES)
        for a in range(a0, a1):
            cb = [jnp.broadcast_to(c_ref[0, hd, a:a + 1, ls], (BF16_ROWS, LANES)).astype(BF16)
                  for hd in range(PEER_HEADS)]
            wb = [jnp.broadcast_to(w1_ref[0, hd, a:a + 1, ls], (BF16_ROWS, LANES)).astype(BF16)
                  for hd in range(PEER_HEADS)]
            for c2 in range(N_KEYS // BF16_ROWS):
                gate = None
                for hd in range(PEER_HEADS):
                    rs = slice(hd * PACKED_KEYS + c2 * SUBLANES, hd * PACKED_KEYS + (c2 + 1) * SUBLANES)
                    r2 = pltpu.bitcast(r2_ref[rs, ls], BF16)
                    e2 = pltpu.bitcast(e2_ref[rs, ls], BF16)
                    term = jnp.where(r2 < cb[hd], e2, jnp.zeros((), BF16)) * wb[hd]
                    gate = term if gate is None else gate + term
                es = slice(a * N_KEYS + c2 * BF16_ROWS, a * N_KEYS + (c2 + 1) * BF16_ROWS)
                act = act_r[es, ls]
                gl = act * (1.0 + lax.erf(act * (1.0 / math.sqrt(2.0))))
                g_w[es, ls] = gl.astype(BF16) * gate


def _expert_call(layer, h2, u, v, r2, e2, c, w1):
    n_tok, d = h2.shape
    lane_spec = pl.BlockSpec((PEER_HEADS * PACKED_KEYS, EXP_TM), lambda t, e: (0, t), pipeline_mode=pl.Buffered(1))
    row_spec = pl.BlockSpec((1, PEER_HEADS, I1_PER_TILE, EXP_TM), lambda t, e: (e, 0, 0, t))
    return pl.pallas_call(
        _expert_kernel,
        grid=(n_tok // EXP_TM, N_EXPERTS // EXP_TE),
        in_specs=[
            pl.BlockSpec((EXP_TM, d), lambda t, e: (t, 0), pipeline_mode=pl.Buffered(1)),
            pl.BlockSpec((1, EXP_TE, d), lambda t, e: (layer, e, 0)),
            pl.BlockSpec((1, EXP_TE, d), lambda t, e: (layer, e, 0)),
            lane_spec, lane_spec, row_spec, row_spec,
        ],
        out_specs=pl.BlockSpec((EXP_TM, d), lambda t, e: (t, 0)),
        out_shape=jax.ShapeDtypeStruct((n_tok, d), F32),
        scratch_shapes=[pltpu.VMEM((EXP_TE, EXP_TM), F32), pltpu.VMEM((EXP_TE, EXP_TM), BF16),
                        pltpu.VMEM((d, EXP_TM), F32)],
        compiler_params=pltpu.CompilerParams(
            dimension_semantics=("arbitrary", "arbitrary"), vmem_limit_bytes=58 * 2**20),
        name="peer_experts",
    )(h2, u, v, r2, e2, c, w1)


def _final_kernel(x_ref, acc_ref, mod_ref, o_ref):
    o_ref[0] = x_ref[0] + mod_ref[0, 5:6, :] * acc_ref[0]


def _final_call(x1, acc, mod):
    batch, seq, d = x1.shape
    tok_spec = pl.BlockSpec((1, FINAL_BLOCK, d), lambda b, i: (b, i, 0))
    return pl.pallas_call(
        _final_kernel,
        grid=(batch, seq // FINAL_BLOCK),
        in_specs=[tok_spec, tok_spec, pl.BlockSpec((1, 6, d), lambda b, i: (b, 0, 0))],
        out_specs=tok_spec,
        out_shape=jax.ShapeDtypeStruct((batch, seq, d), F32),
        compiler_params=pltpu.CompilerParams(dimension_semantics=("arbitrary", "arbitrary")),
        name="final_residual",
    )(x1, acc, mod)


def kernel(x, c, rel_bias, ada_w, ada_b, norm1_gain, w_in, q_gain, k_gain, sinks, w_pool, pool_scale, w_out,
           norm2_gain, peer_w_query, peer_sub_keys, peer_u, peer_v):
    depth = ada_w.shape[0]
    batch, seq, d = x.shape
    mod = _ada_call(c, ada_w, ada_b).reshape(depth, batch, 6, d)
    bias = _bias_call(rel_bias)
    acc = None
    for l in range(depth):
        x = _mix_call(
            x, acc, mod[l - 1] if l > 0 else None, mod[l], norm1_gain[l].reshape(1, d), w_in[l].astype(BF16),
            q_gain[l].reshape(1, HEAD_DIM), k_gain[l].reshape(1, HEAD_DIM), sinks[l], bias,
            w_pool[l].astype(BF16), pool_scale[l].reshape(1, D_POOL), w_out[l].astype(BF16))
        h2, r2, e2, cnt, w1 = _route_call(
            x, mod[l], norm2_gain[l].reshape(1, d), peer_w_query[l].astype(BF16), peer_sub_keys[l].astype(BF16))
        acc = _expert_call(l, h2.reshape(batch * seq, d), peer_u, peer_v, r2, e2, cnt, w1)
        acc = acc.reshape(batch, seq, d)
    return _final_call(x, acc, mod[depth - 1])
```

```python
import functools
import math

import jax
import jax.numpy as jnp
import numpy as np
from jax import lax
from jax.experimental import pallas as pl
from jax.experimental.pallas import tpu as pltpu

F32 = jnp.float32
BF16 = jnp.bfloat16

D_MODEL = 2048
CHUNK = 64
D_POOL = 1024
POOL_WINDOWS = (2, 4, 8, 16)
POOL_GROUP = D_POOL // len(POOL_WINDOWS)
D_ATTN = 1024
HEAD_DIM = 64
N_Q_HEADS = 16
N_KV_HEADS = 2
Q_PER_KV = N_Q_HEADS // N_KV_HEADS
WINDOW = 128
N_BUCKETS = 32
MAX_DISTANCE = 128
D_KV = N_KV_HEADS * HEAD_DIM
D_IN = D_POOL + D_ATTN + 2 * D_KV
PEER_HEADS = 8
N_KEYS = 128
N_EXPERTS = N_KEYS * N_KEYS
PEER_TOPK = 16
D_QUERY = 256
D_HALF = D_QUERY // 2
EPS = 1e-6

LANES = 128
SUBLANES = 8

ADA_TN = 1024
Q_BLOCK = 256
K_BLOCK = Q_BLOCK + WINDOW
POOL_HALO = 16
ROUTE_BLOCK = 512
EXP_TM = 1024
EXP_TE = 512
I1_PER_TILE = EXP_TE // N_KEYS
BF16_ROWS = 2 * SUBLANES
PACKED_KEYS = N_KEYS // 2
EXP_KCHUNK = 512
FINAL_BLOCK = 512

NEG_INF = float("-inf")


def _rms(xf, gain):
    return xf * lax.rsqrt(jnp.mean(xf * xf, axis=-1, keepdims=True) + EPS) * gain


def _ada_kernel(ct_ref, w_ref, b_ref, o_ref):
    w = w_ref[0]
    n_batch = ct_ref.shape[1]
    for b in range(n_batch):
        cb = ct_ref[:, b:b + 1]
        cb = cb * jax.nn.sigmoid(cb)
        o_ref[0, b:b + 1, :] = jnp.sum(w * cb, axis=0, keepdims=True) + b_ref[0]


def _ada_call(c, ada_w, ada_b):
    depth, d, n = ada_w.shape
    batch = c.shape[0]
    return pl.pallas_call(
        _ada_kernel,
        grid=(depth, n // ADA_TN),
        in_specs=[
            pl.BlockSpec((d, batch), lambda l, j: (0, 0)),
            pl.BlockSpec((1, d, ADA_TN), lambda l, j: (l, 0, j)),
            pl.BlockSpec((1, 1, ADA_TN), lambda l, j: (l, 0, j)),
        ],
        out_specs=pl.BlockSpec((1, batch, ADA_TN), lambda l, j: (l, 0, j)),
        out_shape=jax.ShapeDtypeStruct((depth, batch, n), F32),
        compiler_params=pltpu.CompilerParams(
            dimension_semantics=("arbitrary", "arbitrary"), vmem_limit_bytes=40 * 2**20),
        name="ada_mod",
    )(c.T, ada_w, ada_b.reshape(depth, 1, n))


def _t5_bucket(rel):
    nb = N_BUCKETS // 2
    max_exact = nb // 2
    ret = jnp.where(rel > 0, nb, 0)
    n = jnp.abs(rel)
    nf = jnp.maximum(n, 1).astype(jnp.float32)
    large = max_exact + (jnp.log(nf / max_exact) / math.log(MAX_DISTANCE / max_exact) * (nb - max_exact)).astype(jnp.int32)
    large = jnp.minimum(large, nb - 1)
    return ret + jnp.where(n < max_exact, n, large)


def _bias_kernel(rb_ref, bucket_ref, valid_ref, o_ref):
    h = pl.program_id(0)
    bucket = bucket_ref[...]
    acc = jnp.zeros(bucket.shape, F32)
    for b in range(N_BUCKETS):
        acc = jnp.where(bucket == b, rb_ref[b, h], acc)
    o_ref[0] = jnp.where(valid_ref[...] > 0, acc, NEG_INF)


def _bias_call(rel_bias):
    a = jnp.arange(Q_BLOCK)[:, None]
    bk = jnp.arange(K_BLOCK)[None, :]
    rel = bk - WINDOW - a
    bucket = _t5_bucket(rel).astype(jnp.int32)
    dchunk = (bk - WINDOW) // CHUNK - a // CHUNK
    valid = ((dchunk >= -(WINDOW // CHUNK)) & (dchunk <= 0)).astype(jnp.int32)
    return pl.pallas_call(
        _bias_kernel,
        grid=(N_Q_HEADS,),
        in_specs=[
            pl.BlockSpec(memory_space=pltpu.SMEM),
            pl.BlockSpec((Q_BLOCK, K_BLOCK), lambda h: (0, 0)),
            pl.BlockSpec((Q_BLOCK, K_BLOCK), lambda h: (0, 0)),
        ],
        out_specs=pl.BlockSpec((1, Q_BLOCK, K_BLOCK), lambda h: (h, 0, 0)),
        out_shape=jax.ShapeDtypeStruct((N_Q_HEADS, Q_BLOCK, K_BLOCK), F32),
        compiler_params=pltpu.CompilerParams(dimension_semantics=("arbitrary",)),
        name="rel_bias_tile",
    )(rel_bias.astype(F32), bucket, valid)


def _mix_kernel(has_acc, *refs):
    if has_acc:
        (x_ref, acc_ref, modp_ref, mod_ref, n1g_ref, w_in_ref, qg_ref, kg_ref, sinks_ref, bias_ref,
         w_pool_ref, ps_ref, w_out_ref, o_ref, p_ext, k_ext, v_ext, y_scr, s_scr, e_scr, qn_scr) = refs
    else:
        (x_ref, mod_ref, n1g_ref, w_in_ref, qg_ref, kg_ref, sinks_ref, bias_ref,
         w_pool_ref, ps_ref, w_out_ref, o_ref, p_ext, k_ext, v_ext, y_scr, s_scr, e_scr, qn_scr) = refs
    i = pl.program_id(1)
    x = x_ref[0]
    if has_acc:
        x = x + modp_ref[0, 5:6, :] * acc_ref[0]
    sh1 = mod_ref[0, 0:1, :]
    sc1 = mod_ref[0, 1:2, :]
    g1 = mod_ref[0, 2:3, :]
    h = _rms(x, n1g_ref[...]) * (1.0 + sc1) + sh1
    z = jnp.dot(h.astype(BF16), w_in_ref[...], preferred_element_type=F32)

    @pl.when(i == 0)
    def _():
        p_ext[0:POOL_HALO, :] = jnp.zeros((POOL_HALO, D_POOL), F32)
        k_ext[0:WINDOW, :] = jnp.zeros((WINDOW, D_KV), F32)
        v_ext[0:WINDOW, :] = jnp.zeros((WINDOW, D_KV), F32)

    @pl.when(i > 0)
    def _():
        p_ext[0:POOL_HALO, :] = p_ext[Q_BLOCK:Q_BLOCK + POOL_HALO, :]
        k_ext[0:WINDOW, :] = k_ext[Q_BLOCK:Q_BLOCK + WINDOW, :]
        v_ext[0:WINDOW, :] = v_ext[Q_BLOCK:Q_BLOCK + WINDOW, :]

    p_ext[POOL_HALO:POOL_HALO + Q_BLOCK, :] = z[:, 0:D_POOL]
    kg = kg_ref[...]
    for g in range(N_KV_HEADS):
        lo = D_POOL + D_ATTN + g * HEAD_DIM
        k_ext[WINDOW:WINDOW + Q_BLOCK, g * HEAD_DIM:(g + 1) * HEAD_DIM] = _rms(z[:, lo:lo + HEAD_DIM], kg)
    v_ext[WINDOW:WINDOW + Q_BLOCK, :] = z[:, D_POOL + D_ATTN + D_KV:D_IN]

    tpos = i * Q_BLOCK + lax.broadcasted_iota(jnp.int32, (Q_BLOCK, 1), 0)
    for g, w in enumerate(POOL_WINDOWS):
        c0 = g * POOL_GROUP
        cur = p_ext[POOL_HALO:POOL_HALO + Q_BLOCK, c0:c0 + POOL_GROUP]
        acc = cur
        for d in range(1, w):
            acc = acc + p_ext[POOL_HALO - d:POOL_HALO - d + Q_BLOCK, c0:c0 + POOL_GROUP]
        cnt = jnp.minimum(tpos + 1, w).astype(F32)
        pooled = acc / cnt - cur
        y = jnp.dot(pooled.astype(BF16), w_pool_ref[g], preferred_element_type=F32)
        y_scr[:, c0:c0 + POOL_GROUP] = y * ps_ref[:, c0:c0 + POOL_GROUP]

    qg = qg_ref[...]
    first_cols = lax.broadcasted_iota(jnp.int32, (Q_BLOCK, K_BLOCK), 1) < WINDOW
    hide = jnp.logical_and(first_cols, i == 0)
    kn = [k_ext[:, g * HEAD_DIM:(g + 1) * HEAD_DIM].astype(BF16) for g in range(N_KV_HEADS)]
    vv = [v_ext[:, g * HEAD_DIM:(g + 1) * HEAD_DIM].astype(BF16) for g in range(N_KV_HEADS)]
    for hq in range(N_Q_HEADS):
        lo = D_POOL + hq * HEAD_DIM
        qn_scr[:, hq * HEAD_DIM:(hq + 1) * HEAD_DIM] = _rms(z[:, lo:lo + HEAD_DIM], qg).astype(BF16)
    for hq in range(N_Q_HEADS):
        qh = qn_scr[:, hq * HEAD_DIM:(hq + 1) * HEAD_DIM]
        s = lax.dot_general(qh, kn[hq // Q_PER_KV], (((1,), (1,)), ((), ())), preferred_element_type=F32)
        s = s * (1.0 / math.sqrt(HEAD_DIM)) + bias_ref[hq]
        s_scr[hq] = jnp.where(hide, NEG_INF, s)
    for hq in range(N_Q_HEADS):
        s = s_scr[hq]
        sink = sinks_ref[hq]
        m = jnp.maximum(jnp.max(s, axis=-1, keepdims=True), sink)
        e = jnp.exp(s - m)
        den = jnp.sum(e, axis=-1, keepdims=True) + jnp.exp(sink - m)
        e_scr[hq] = (e * (1.0 / den)).astype(BF16)
    for hq in range(N_Q_HEADS):
        o = jnp.dot(e_scr[hq], vv[hq // Q_PER_KV], preferred_element_type=F32)
        y_scr[:, D_POOL + hq * HEAD_DIM:D_POOL + (hq + 1) * HEAD_DIM] = o

    out = jnp.dot(y_scr[...].astype(BF16), w_out_ref[...], preferred_element_type=F32)
    o_ref[0] = x + g1 * out


def _const_spec(shape):
    nd = len(shape)
    return pl.BlockSpec(shape, lambda *_: (0,) * nd, pipeline_mode=pl.Buffered(1))


def _mix_call(x, acc, mod_prev, mod, n1g, w_in, qg, kg, sinks, bias, w_pool, ps, w_out):
    batch, seq, d = x.shape
    has_acc = acc is not None
    tok_spec = pl.BlockSpec((1, Q_BLOCK, d), lambda b, i: (b, i, 0))
    mod_spec = pl.BlockSpec((1, 6, d), lambda b, i: (b, 0, 0))
    in_specs = [tok_spec]
    args = [x]
    if has_acc:
        in_specs += [tok_spec, mod_spec]
        args += [acc, mod_prev]
    in_specs += [
        mod_spec,
        _const_spec((1, d)),
        _const_spec((d, D_IN)),
        _const_spec((1, HEAD_DIM)),
        _const_spec((1, HEAD_DIM)),
        pl.BlockSpec(memory_space=pltpu.SMEM),
        _const_spec((N_Q_HEADS, Q_BLOCK, K_BLOCK)),
        _const_spec((len(POOL_WINDOWS), POOL_GROUP, POOL_GROUP)),
        _const_spec((1, D_POOL)),
        _const_spec((d, d)),
    ]
    args += [mod, n1g, w_in, qg, kg, sinks, bias, w_pool, ps, w_out]
    return pl.pallas_call(
        functools.partial(_mix_kernel, has_acc),
        grid=(batch, seq // Q_BLOCK),
        in_specs=in_specs,
        out_specs=tok_spec,
        out_shape=jax.ShapeDtypeStruct((batch, seq, d), F32),
        scratch_shapes=[
            pltpu.VMEM((POOL_HALO + Q_BLOCK, D_POOL), F32),
            pltpu.VMEM((K_BLOCK, D_KV), F32),
            pltpu.VMEM((K_BLOCK, D_KV), F32),
            pltpu.VMEM((Q_BLOCK, d), F32),
            pltpu.VMEM((N_Q_HEADS, Q_BLOCK, K_BLOCK), F32),
            pltpu.VMEM((N_Q_HEADS, Q_BLOCK, K_BLOCK), BF16),
            pltpu.VMEM((Q_BLOCK, D_ATTN), BF16),
        ],
        compiler_params=pltpu.CompilerParams(
            dimension_semantics=("arbitrary", "arbitrary"), vmem_limit_bytes=56 * 2**20),
        name="mixer",
    )(*args)


def _top16_rows(s, exact, want_rank=True):
    rows = lax.broadcasted_iota(jnp.int32, s.shape, 0).astype(F32)
    rank = jnp.full(s.shape, float(PEER_TOPK), F32) if want_rank else None
    cur = s
    vals = []
    for r in range(PEER_TOPK):
        m = jnp.max(cur, axis=0, keepdims=True)
        sel = cur == m
        if exact:
            first = jnp.min(jnp.where(sel, rows, float(N_KEYS)), axis=0, keepdims=True)
            sel = rows == first
        if want_rank:
            rank = jnp.where(sel, float(r), rank)
        cur = jnp.where(sel, NEG_INF, cur)
        vals.append(m)
    return rank, vals


def _stack16(vals):
    t = vals[0].shape[1]
    rows = lax.broadcasted_iota(jnp.int32, (PEER_TOPK, t), 0)
    out = jnp.zeros((PEER_TOPK, t), F32)
    for j, v in enumerate(vals):
        out = jnp.where(rows == j, v, out)
    return out


def _select_counts(xs, ys, exact):
    t = xs[0].shape[1]
    rows = lax.broadcasted_iota(jnp.int32, (PEER_TOPK, t), 0)
    rows_f = rows.astype(F32)
    x16 = _stack16(xs)
    y16 = _stack16(ys)
    cur, flat = [], []
    n_a = 4
    big = float(PEER_TOPK * PEER_TOPK)
    for k in range(n_a):
        ok = rows < PEER_TOPK // (k + 1)
        cur.append(jnp.where(ok, x16 + ys[k], NEG_INF))
        flat.append(jnp.where(ok, rows_f * float(PEER_TOPK) + float(k), big + 1.0))
    for j in range(3):
        ok = jnp.logical_and(rows >= n_a, rows < PEER_TOPK // (j + 1))
        cur.append(jnp.where(ok, y16 + xs[j], NEG_INF))
        flat.append(jnp.where(ok, rows_f + float(PEER_TOPK * j), big + 1.0))
    taken = [jnp.zeros((PEER_TOPK, t), F32) for _ in cur]
    top = xs[0] + ys[0]
    z = jnp.zeros((1, t), F32)
    for _ in range(PEER_TOPK):
        m = cur[0]
        for c in cur[1:]:
            m = jnp.maximum(m, c)
        m = jnp.max(m, axis=0, keepdims=True)
        if exact:
            f = jnp.where(cur[0] == m, flat[0], big)
            for c, fl in zip(cur[1:], flat[1:]):
                f = jnp.minimum(f, jnp.where(c == m, fl, big))
            f = jnp.min(f, axis=0, keepdims=True)
        for n in range(len(cur)):
            sel = (flat[n] == f) if exact else (cur[n] == m)
            cur[n] = jnp.where(sel, NEG_INF, cur[n])
            taken[n] = jnp.where(sel, 1.0, taken[n])
        z = z + jnp.exp(m - top)
    cnt = taken[0]
    for k in range(1, n_a):
        cnt = cnt + taken[k]
    for j in range(3):
        cnt = cnt + jnp.where(rows == j, jnp.sum(taken[n_a + j], axis=0, keepdims=True), 0.0)
    return cnt, z


def _route_kernel(x_ref, mod_ref, n2g_ref, wq_ref, keys_ref, h2_ref, r2_ref, e2_ref, c_ref, w1_ref, q_scr):
    x = x_ref[0]
    sh2 = mod_ref[0, 3:4, :]
    sc2 = mod_ref[0, 4:5, :]
    h2 = (_rms(x, n2g_ref[...]) * (1.0 + sc2) + sh2).astype(BF16)
    h2_ref[0] = h2
    q_scr[...] = jnp.dot(h2, wq_ref[...], preferred_element_type=F32)
    k1 = keys_ref[0]
    k2 = keys_ref[1]
    t = q_scr.shape[0]

    def head(hd, carry):
        off = pl.multiple_of(hd * D_QUERY, D_QUERY)
        q1 = q_scr[:, pl.ds(off, D_HALF)].astype(BF16)
        q2 = q_scr[:, pl.ds(off + D_HALF, D_HALF)].astype(BF16)
        dn = (((1,), (1,)), ((), ()))
        s1 = lax.dot_general(k1, q1, dn, preferred_element_type=F32)
        s2 = lax.dot_general(k2, q2, dn, preferred_element_type=F32)
        def route(exact):
            r1, xs = _top16_rows(s1, exact, want_rank=exact)
            r2, ys = _top16_rows(s2, exact)
            cnt, z = _select_counts(xs, ys, exact)
            c_full = jnp.zeros(s1.shape, F32)
            for j in range(PEER_TOPK):
                hit = (r1 == float(j)) if exact else (s1 == xs[j])
                c_full = jnp.where(hit, cnt[j:j + 1, :], c_full)
            w1 = jnp.exp(s1 - xs[0]) * (0.5 / z)
            head_rows = pl.ds(pl.multiple_of(hd * PACKED_KEYS, PACKED_KEYS), PACKED_KEYS)
            r2_ref[head_rows, :] = pltpu.bitcast(r2.astype(BF16), jnp.uint32)
            e2_ref[head_rows, :] = pltpu.bitcast(jnp.exp(s2 - ys[0]).astype(BF16), jnp.uint32)
            for blk in range(N_KEYS // I1_PER_TILE):
                rows = slice(blk * I1_PER_TILE, (blk + 1) * I1_PER_TILE)
                c_ref[blk, hd] = c_full[rows, :]
                w1_ref[blk, hd] = w1[rows, :]
            return xs, r2, cnt

        xs, r2, cnt = route(False)
        k16 = float(PEER_TOPK)
        n1 = jnp.sum(jnp.where(s1 >= xs[PEER_TOPK - 1], 1.0, 0.0), axis=0, keepdims=True)
        n2 = jnp.sum(jnp.where(r2 < k16, 1.0, 0.0), axis=0, keepdims=True)
        nc = jnp.sum(cnt, axis=0, keepdims=True)
        ok = jnp.logical_and(jnp.logical_and(n1 == k16, n2 == k16), nc == k16)
        n_bad = jnp.sum(jnp.where(ok, 0.0, 1.0))

        @pl.when(n_bad > 0.0)
        def _():
            route(True)

        return carry

    lax.fori_loop(0, PEER_HEADS, head, 0)


def _route_call(x1, mod, n2g, wq, keys):
    batch, seq, d = x1.shape
    n_tok = batch * seq
    nblk = seq // ROUTE_BLOCK
    tok_spec = pl.BlockSpec((1, ROUTE_BLOCK, d), lambda b, i: (b, i, 0))
    lane_spec = pl.BlockSpec((PEER_HEADS * PACKED_KEYS, ROUTE_BLOCK), lambda b, i: (0, b * nblk + i))
    row_spec = pl.BlockSpec((N_KEYS // I1_PER_TILE, PEER_HEADS, I1_PER_TILE, ROUTE_BLOCK),
                            lambda b, i: (0, 0, 0, b * nblk + i))
    lane_shape = jax.ShapeDtypeStruct((PEER_HEADS * PACKED_KEYS, n_tok), jnp.uint32)
    row_shape = jax.ShapeDtypeStruct((N_KEYS // I1_PER_TILE, PEER_HEADS, I1_PER_TILE, n_tok), F32)
    return pl.pallas_call(
        _route_kernel,
        grid=(batch, nblk),
        in_specs=[
            tok_spec,
            pl.BlockSpec((1, 6, d), lambda b, i: (b, 0, 0)),
            _const_spec((1, d)),
            _const_spec((d, PEER_HEADS * D_QUERY)),
            _const_spec((2, N_KEYS, D_HALF)),
        ],
        out_specs=[tok_spec, lane_spec, lane_spec, row_spec, row_spec],
        out_shape=[jax.ShapeDtypeStruct((batch, seq, d), BF16), lane_shape, lane_shape, row_shape, row_shape],
        scratch_shapes=[pltpu.VMEM((ROUTE_BLOCK, PEER_HEADS * D_QUERY), F32)],
        compiler_params=pltpu.CompilerParams(
            dimension_semantics=("arbitrary", "arbitrary"), vmem_limit_bytes=48 * 2**20),
        name="peer_route",
    )(x1, mod, n2g, wq, keys)


def _expert_kernel(h2_ref, u_ref, v_ref, r2_ref, e2_ref, c_ref, w1_ref, o_ref, act_scr, g_scr):
    e = pl.program_id(1)

    @pl.when(e == 0)
    def _():
        o_ref[...] = jnp.zeros(o_ref.shape, F32)

    act_scr[...] = lax.dot_general(u_ref[0].astype(BF16), h2_ref[...], (((1,), (1,)), ((), ())),
                                   preferred_element_type=F32)
    _gate_chunk(r2_ref, e2_ref, c_ref, w1_ref, act_scr, g_scr, 0, I1_PER_TILE)
    o_ref[...] += lax.dot_general(g_scr[...], v_ref[0].astype(BF16), (((0,), (0,)), ((), ())),
                                  preferred_element_type=F32)


def _gate_chunk(r2_ref, e2_ref, c_ref, w1_ref, act_r, g_w, a0, a1):
    for lc in range(EXP_TM // LANES):
        ls = slice(lc * LANES, (lc + 1) * LANES)
        for a in range(a0, a1):
            cb = [jnp.broadcast_to(c_ref[0, hd, a:a + 1, ls], (BF16_ROWS, LANES)).astype(BF16)
                  for hd in range(PEER_HEADS)]
            wb = [jnp.broadcast_to(w1_ref[0, hd, a:a + 1, ls], (BF16_ROWS, LANES)).astype(BF16)
                  for hd in range(PEER_HEADS)]
            for c2 in range(N_KEYS // BF16_ROWS):
                gate = None
                for hd in range(PEER_HEADS):
                    rs = slice(hd * PACKED_KEYS + c2 * SUBLANES, hd * PACKED_KEYS + (c2 + 1) * SUBLANES)
                    r2 = pltpu.bitcast(r2_ref[rs, ls], BF16)
                    e2 = pltpu.bitcast(e2_ref[rs, ls], BF16)
                    term = jnp.where(r2 < cb[hd], e2, jnp.zeros((), BF16)) * wb[hd]
                    gate = term if gate is None else gate + term
                es = slice(a * N_KEYS + c2 * BF16_ROWS, a * N_KEYS + (c2 + 1) * BF16_ROWS)
                act = act_r[es, ls]
                gl = act * (1.0 + lax.erf(act * (1.0 / math.sqrt(2.0))))
                g_w[es, ls] = gl.astype(BF16) * gate


def _expert_call(layer, h2, u, v, r2, e2, c, w1):
    n_tok, d = h2.shape
    lane_spec = pl.BlockSpec((PEER_HEADS * PACKED_KEYS, EXP_TM), lambda t, e: (0, t), pipeline_mode=pl.Buffered(1))
    row_spec = pl.BlockSpec((1, PEER_HEADS, I1_PER_TILE, EXP_TM), lambda t, e: (e, 0, 0, t))
    return pl.pallas_call(
        _expert_kernel,
        grid=(n_tok // EXP_TM, N_EXPERTS // EXP_TE),
        in_specs=[
            pl.BlockSpec((EXP_TM, d), lambda t, e: (t, 0), pipeline_mode=pl.Buffered(1)),
            pl.BlockSpec((1, EXP_TE, d), lambda t, e: (layer, e, 0)),
            pl.BlockSpec((1, EXP_TE, d), lambda t, e: (layer, e, 0)),
            lane_spec, lane_spec, row_spec, row_spec,
        ],
        out_specs=pl.BlockSpec((EXP_TM, d), lambda t, e: (t, 0)),
        out_shape=jax.ShapeDtypeStruct((n_tok, d), F32),
        scratch_shapes=[pltpu.VMEM((EXP_TE, EXP_TM), F32), pltpu.VMEM((EXP_TE, EXP_TM), BF16)],
        compiler_params=pltpu.CompilerParams(
            dimension_semantics=("arbitrary", "arbitrary"), vmem_limit_bytes=58 * 2**20),
        name="peer_experts",
    )(h2, u, v, r2, e2, c, w1)


def _final_kernel(x_ref, acc_ref, mod_ref, o_ref):
    o_ref[0] = x_ref[0] + mod_ref[0, 5:6, :] * acc_ref[0]


def _final_call(x1, acc, mod):
    batch, seq, d = x1.shape
    tok_spec = pl.BlockSpec((1, FINAL_BLOCK, d), lambda b, i: (b, i, 0))
    return pl.pallas_call(
        _final_kernel,
        grid=(batch, seq // FINAL_BLOCK),
        in_specs=[tok_spec, tok_spec, pl.BlockSpec((1, 6, d), lambda b, i: (b, 0, 0))],
        out_specs=tok_spec,
        out_shape=jax.ShapeDtypeStruct((batch, seq, d), F32),
        compiler_params=pltpu.CompilerParams(dimension_semantics=("arbitrary", "arbitrary")),
        name="final_residual",
    )(x1, acc, mod)


def kernel(x, c, rel_bias, ada_w, ada_b, norm1_gain, w_in, q_gain, k_gain, sinks, w_pool, pool_scale, w_out,
           norm2_gain, peer_w_query, peer_sub_keys, peer_u, peer_v):
    depth = ada_w.shape[0]
    batch, seq, d = x.shape
    mod = _ada_call(c, ada_w, ada_b).reshape(depth, batch, 6, d)
    bias = _bias_call(rel_bias)
    acc = None
    for l in range(depth):
        x = _mix_call(
            x, acc, mod[l - 1] if l > 0 else None, mod[l], norm1_gain[l].reshape(1, d), w_in[l].astype(BF16),
            q_gain[l].reshape(1, HEAD_DIM), k_gain[l].reshape(1, HEAD_DIM), sinks[l], bias,
            w_pool[l].astype(BF16), pool_scale[l].reshape(1, D_POOL), w_out[l].astype(BF16))
        h2, r2, e2, cnt, w1 = _route_call(
            x, mod[l], norm2_gain[l].reshape(1, d), peer_w_query[l].astype(BF16), peer_sub_keys[l].astype(BF16))
        acc = _expert_call(l, h2.reshape(batch * seq, d), peer_u, peer_v, r2, e2, cnt, w1)
        acc = acc.reshape(batch, seq, d)
    return _final_call(x, acc, mod[depth - 1])
```
